```python
import math
import jax, jax.numpy as jnp
from jax import lax
import numpy as np

D_MODEL = 1024
BATCH = 1
SEQ = 16384
DEPTH = 4
DEC_BATCH = 16
DEC_SEQ = 64
PAST_LEN = 4096

CHUNK = 64
Q_BLOCK = 128
MIX_WIDTH = D_MODEL
GLA_WIDTH = MIX_WIDTH // 2
GLA_HEADS = 4
GLA_DV = GLA_WIDTH // GLA_HEADS
GLA_DK = GLA_DV // 2
GLA_GATE_RANK = 16
GLA_GATE_NORMALIZER = 16.0
DIFF_WIDTH = MIX_WIDTH - GLA_WIDTH
DIFF_HEADS = 4
DIFF_DV = DIFF_WIDTH // DIFF_HEADS
DIFF_HEAD_DIM = DIFF_DV // 2
DIFF_KDIM = 2 * DIFF_HEAD_DIM
T5_BUCKETS = 32
T5_MAX_DIST = 128
D_FF = ((8 * D_MODEL // 3 + 255) // 256) * 256
NEG_INF = -1e30
RMS_EPS = 1e-6
IN_SPLITS = (GLA_HEADS * GLA_DK, GLA_HEADS * GLA_DK, GLA_WIDTH, GLA_WIDTH, GLA_GATE_RANK,
             DIFF_HEADS * DIFF_KDIM, DIFF_HEADS * DIFF_KDIM, DIFF_WIDTH)
D_IN = 2 * GLA_HEADS * GLA_DK + 2 * GLA_WIDTH + GLA_GATE_RANK + 2 * DIFF_HEADS * DIFF_KDIM + DIFF_WIDTH

kernel_name = 'hymba_gla_diffattn_streaming_step'


def split_points():
    pts, acc = [], 0
    for s in IN_SPLITS[:-1]:
        acc += s
        pts.append(acc)
    return pts


def rmsnorm(x, g):
    xf = x.astype(jnp.float32)
    y = xf * lax.rsqrt(jnp.mean(xf * xf, axis=-1, keepdims=True) + RMS_EPS)
    return (y * g.astype(jnp.float32)).astype(x.dtype)


def t5_bucket(rel):
    nb = T5_BUCKETS // 2
    max_exact = nb // 2
    ret = jnp.where(rel > 0, nb, 0)
    n = jnp.abs(rel)
    nf = jnp.maximum(n, 1).astype(jnp.float32)
    large = max_exact + (jnp.log(nf / max_exact) / math.log(T5_MAX_DIST / max_exact)
                         * (nb - max_exact)).astype(jnp.int32)
    large = jnp.minimum(large, nb - 1)
    return ret + jnp.where(n < max_exact, n, large)


def gla_chunk(S, q, k, v, log_a):
    L = q.shape[1]
    b = jnp.cumsum(log_a.astype(jnp.float32), axis=1)
    causal = jnp.tril(jnp.ones((L, L), dtype=bool))
    rel = b[:, :, None] - b[:, None, :]
    decay = jnp.exp(jnp.where(causal[None, :, :, None, None], rel, NEG_INF))
    scores = jnp.einsum('bthd,bshd,btshd->bhts', q, k, decay)
    o_intra = jnp.einsum('bhts,bshv->bthv', scores, v)
    o_inter = jnp.einsum('bthd,bhdv->bthv', q * jnp.exp(b), S)
    b_last = b[:, -1]
    k_dec = k * jnp.exp(b_last[:, None] - b)
    S_new = jnp.exp(b_last)[..., None] * S + jnp.einsum('bshd,bshv->bhdv', k_dec, v)
    return S_new.astype(S.dtype), (o_intra + o_inter).astype(jnp.float32)


def gla_prompt(q, k, v, log_a):
    B, T = q.shape[:2]
    n = T // CHUNK

    def to_chunks(a):
        return jnp.moveaxis(a.reshape(B, n, CHUNK, *a.shape[2:]), 1, 0)

    S0 = jnp.zeros((B, GLA_HEADS, GLA_DK, GLA_DV), jnp.float32)
    S_fin, o = lax.scan(lambda S, c: gla_chunk(S, *c), S0,
                        (to_chunks(q), to_chunks(k), to_chunks(v), to_chunks(log_a)))
    o = jnp.moveaxis(o, 0, 1).reshape(B, T, GLA_HEADS, GLA_DV)
    return o, S_fin


def diff_attend(q, k, v, qpos, kpos, lam, rel_bias):
    logits = jnp.einsum('bqhmd,bkhmd->bhmqk', q, k).astype(jnp.float32) * (DIFF_HEAD_DIM ** -0.5)
    bias = jnp.transpose(rel_bias[t5_bucket(kpos[None, :] - qpos[:, None])].astype(jnp.float32), (2, 0, 1))
    allowed = (kpos[None, :] // CHUNK) <= (qpos[:, None] // CHUNK)
    logits = jnp.where(allowed, logits + bias[None, :, None], NEG_INF)
    p = jax.nn.softmax(logits, axis=-1)
    a = p[:, :, 0] - lam * p[:, :, 1]
    return jnp.einsum('bhqk,bkhv->bqhv', a, v.astype(jnp.float32))


def diff_prompt(q, k, v, lam, rel_bias):
    B, T = q.shape[:2]
    nb = T // Q_BLOCK
    qb = jnp.moveaxis(q.reshape(B, nb, Q_BLOCK, *q.shape[2:]), 1, 0)
    starts = jnp.arange(nb, dtype=jnp.int32) * Q_BLOCK
    kpos = jnp.arange(T, dtype=jnp.int32)

    def block(args):
        qblk, s = args
        return diff_attend(qblk, k, v, s + jnp.arange(Q_BLOCK, dtype=jnp.int32), kpos, lam, rel_bias)

    o = lax.map(block, (qb, starts))
    return jnp.moveaxis(o, 0, 1).reshape(B, T, DIFF_HEADS, DIFF_DV)


def run_trunk(x, past_k, past_v, gla_state, w_in, gla_w_alpha2, gla_b_alpha, gla_norm_g,
              diff_lambda, diff_norm_g, w_out, norm_mix_g, norm_ffn_g, w_ffn_in, w_ffn_out,
              final_norm_g, rel_bias):
    B, T, _ = x.shape
    new_k, new_v, new_s = [], [], []
    for l in range(DEPTH):
        xn = rmsnorm(x, norm_mix_g[l])
        gq, gk, gv, gg, glr, dq, dk, dv = jnp.split(xn @ w_in[l], split_points(), axis=-1)
        q = gq.reshape(B, T, GLA_HEADS, GLA_DK) * (GLA_DK ** -0.5)
        k = gk.reshape(B, T, GLA_HEADS, GLA_DK)
        v = gv.reshape(B, T, GLA_HEADS, GLA_DV)
        z = (glr @ gla_w_alpha2[l] + gla_b_alpha[l]).astype(jnp.float32)
        log_a = (jax.nn.log_sigmoid(z) / GLA_GATE_NORMALIZER).reshape(B, T, GLA_HEADS, GLA_DK)
        if gla_state is None:
            o_gla, s_fin = gla_prompt(q, k, v, log_a)
        else:
            s_fin, o_gla = gla_chunk(gla_state[l], q, k, v, log_a)
        o_gla = rmsnorm(o_gla, gla_norm_g[l]).reshape(B, T, GLA_WIDTH) * jax.nn.silu(gg.astype(jnp.float32))
        lam_init = 0.8 - 0.6 * math.exp(-0.3 * l)
        lp = diff_lambda[l].astype(jnp.float32)
        lam = jnp.exp(jnp.sum(lp[0] * lp[1])) - jnp.exp(jnp.sum(lp[2] * lp[3])) + lam_init
        qd = dq.reshape(B, T, DIFF_HEADS, 2, DIFF_HEAD_DIM)
        kd = dk.reshape(B, T, DIFF_HEADS, 2, DIFF_HEAD_DIM)
        vd = dv.reshape(B, T, DIFF_HEADS, DIFF_DV)
        if past_k is None:
            o_diff = diff_prompt(qd, kd, vd, lam, rel_bias)
        else:
            P = past_k.shape[2]
            k_all = jnp.concatenate([past_k[l].reshape(B, P, DIFF_HEADS, 2, DIFF_HEAD_DIM), kd], axis=1)
            v_all = jnp.concatenate([past_v[l], vd], axis=1)
            qpos = P + jnp.arange(T, dtype=jnp.int32)
            kpos = jnp.arange(P + T, dtype=jnp.int32)
            o_diff = diff_attend(qd, k_all, v_all, qpos, kpos, lam, rel_bias)
        o_diff = rmsnorm(o_diff, diff_norm_g[l]).reshape(B, T, DIFF_WIDTH) * (1.0 - lam_init)
        x = x + jnp.concatenate([o_gla, o_diff], axis=-1).astype(x.dtype) @ w_out[l]
        hn = rmsnorm(x, norm_ffn_g[l])
        gate, up = jnp.split(hn @ w_ffn_in[l], 2, axis=-1)
        x = x + (jax.nn.silu(gate) * up) @ w_ffn_out[l]
        new_k.append(dk.reshape(B, T, DIFF_HEADS, DIFF_KDIM))
        new_v.append(vd)
        new_s.append(s_fin)
    y = rmsnorm(x, final_norm_g)
    return y, jnp.stack(new_k), jnp.stack(new_v), jnp.stack(new_s)


def setup_inputs(seed: int = 0) -> dict:
    key = jax.random.key(seed)
    ks = jax.random.split(key, 18)
    f32 = jnp.float32

    def nrm(k, shape, scale):
        return jax.random.normal(k, shape, f32) * scale

    return {
        'x_prompt': nrm(ks[0], (BATCH, SEQ, D_MODEL), 1.0),
        'x_sample': nrm(ks[1], (DEC_BATCH, DEC_SEQ, D_MODEL), 1.0),
        'cache_diff_k': nrm(ks[2], (DEPTH, DEC_BATCH, PAST_LEN, DIFF_HEADS, DIFF_KDIM), 1.0),
        'cache_diff_v': nrm(ks[3], (DEPTH, DEC_BATCH, PAST_LEN, DIFF_HEADS, DIFF_DV), 1.0),
        'state_gla': nrm(ks[4], (DEPTH, DEC_BATCH, GLA_HEADS, GLA_DK, GLA_DV), 0.5),
        'w_in': nrm(ks[5], (DEPTH, D_MODEL, D_IN), D_MODEL ** -0.5),
        'gla_w_alpha2': nrm(ks[6], (DEPTH, GLA_GATE_RANK, GLA_HEADS * GLA_DK), GLA_GATE_RANK ** -0.5),
        'gla_b_alpha': nrm(ks[7], (DEPTH, GLA_HEADS * GLA_DK), 0.1),
        'gla_norm_g': 1.0 + nrm(ks[8], (DEPTH, GLA_DV), 0.02),
        'diff_lambda': nrm(ks[9], (DEPTH, 4, DIFF_HEAD_DIM), 0.1),
        'diff_norm_g': 1.0 + nrm(ks[10], (DEPTH, DIFF_DV), 0.02),
        'w_out': nrm(ks[11], (DEPTH, MIX_WIDTH, D_MODEL), MIX_WIDTH ** -0.5),
        'norm_mix_g': 1.0 + nrm(ks[12], (DEPTH, D_MODEL), 0.02),
        'norm_ffn_g': 1.0 + nrm(ks[13], (DEPTH, D_MODEL), 0.02),
        'w_ffn_in': nrm(ks[14], (DEPTH, D_MODEL, 2 * D_FF), D_MODEL ** -0.5),
        'w_ffn_out': nrm(ks[15], (DEPTH, D_FF, D_MODEL), D_FF ** -0.5),
        'final_norm_g': 1.0 + nrm(ks[16], (D_MODEL,), 0.02),
        'rel_bias': nrm(ks[17], (T5_BUCKETS, DIFF_HEADS), 0.5),
    }


def reference(x_prompt, x_sample, cache_diff_k, cache_diff_v, state_gla, w_in, gla_w_alpha2,
              gla_b_alpha, gla_norm_g, diff_lambda, diff_norm_g, w_out, norm_mix_g, norm_ffn_g,
              w_ffn_in, w_ffn_out, final_norm_g, rel_bias):
    y_prompt, new_k_prompt, new_v_prompt, new_gla_prompt = run_trunk(
        x_prompt, None, None, None, w_in, gla_w_alpha2, gla_b_alpha, gla_norm_g, diff_lambda,
        diff_norm_g, w_out, norm_mix_g, norm_ffn_g, w_ffn_in, w_ffn_out, final_norm_g, rel_bias)
    y_sample, new_k_sample, new_v_sample, new_gla_sample = run_trunk(
        x_sample, cache_diff_k, cache_diff_v, state_gla, w_in, gla_w_alpha2, gla_b_alpha, gla_norm_g,
        diff_lambda, diff_norm_g, w_out, norm_mix_g, norm_ffn_g, w_ffn_in, w_ffn_out, final_norm_g, rel_bias)
    return (y_prompt, y_sample, new_k_prompt, new_v_prompt, new_gla_prompt,
            new_k_sample, new_v_sample, new_gla_sample)
```

```python
import functools
import math

import jax
import jax.numpy as jnp
from jax import lax
from jax.experimental import pallas as pl
from jax.experimental.pallas import tpu as pltpu

F32 = jnp.float32
BF16 = jnp.bfloat16

D_MODEL = 1024
CHUNK = 64
GLA_HEADS = 4
GLA_DK = 64
GLA_DV = 128
GLA_WIDTH = GLA_HEADS * GLA_DV
GLA_QK = GLA_HEADS * GLA_DK
GLA_GATE_RANK = 16
GLA_GATE_NORMALIZER = 16.0
DIFF_HEADS = 4
DIFF_DV = 128
DIFF_HEAD_DIM = 64
DIFF_WIDTH = DIFF_HEADS * DIFF_DV
T5_BUCKETS = 32
T5_MAX_DIST = 128
D_FF = 2816
NEG_INF = -1e30
RMS_EPS = 1e-6

LANES = 128
GATE_PAD = LANES
VMEM_LIMIT = 56 * 1024 * 1024

_COLS = {}
_off = 0
for _name, _w in (("gq", GLA_QK), ("gk", GLA_QK), ("gv", GLA_WIDTH), ("gg", GLA_WIDTH), ("glr", GATE_PAD),
                  ("dq", DIFF_WIDTH), ("dk", DIFF_WIDTH), ("dv", DIFF_WIDTH)):
    _COLS[_name] = (_off, _w)
    _off += _w
D_IN_PAD = _off

ATT_TQ = 256
ATT_TK = 256
FAR_BLOCKS = 2
GLA_CHUNKS_PER_STEP = 8
FF_CHUNK = 512


def _nt_dot(a, b):
    return lax.dot_general(a, b, (((1,), (1,)), ((), ())), preferred_element_type=F32)


def _rms(x, g):
    return x * lax.rsqrt(jnp.mean(x * x, axis=-1, keepdims=True) + RMS_EPS) * g


def _inproj_kernel(x_ref, g_ref, w_ref, wa_ref, ba_ref,
                   gq_ref, gk_ref, gv_ref, gg_ref, la_ref, dq_ref, dk_ref, dv_ref, *attn_refs):
    xn = _rms(x_ref[...], g_ref[...]).astype(BF16)

    def proj(name):
        c0, w = _COLS[name]
        return jnp.dot(xn, w_ref[:, c0:c0 + w], preferred_element_type=F32)

    gq_ref[...] = proj("gq") * (GLA_DK ** -0.5)
    gk_ref[...] = proj("gk")
    gv_ref[...] = proj("gv")
    gg_ref[...] = proj("gg")
    z = jnp.dot(proj("glr").astype(BF16), wa_ref[...], preferred_element_type=F32) + ba_ref[...]
    la_ref[...] = (jnp.minimum(z, 0.0) - jnp.log1p(jnp.exp(-jnp.abs(z)))) / GLA_GATE_NORMALIZER
    dq_ref[...] = (proj("dq") * (DIFF_HEAD_DIM ** -0.5)).astype(BF16)
    dk = proj("dk")
    dv = proj("dv")
    dk_ref[...] = dk
    dv_ref[...] = dv
    if attn_refs:
        dkb_ref, dvt_ref = attn_refs
        dkb_ref[...] = dk.astype(BF16)
        tm = dv.shape[0]
        for r in range(tm // ATT_TK):
            vt = dv[r * ATT_TK:(r + 1) * ATT_TK, :].T
            for h in range(DIFF_HEADS):
                dvt_ref[h, r] = vt[h * DIFF_DV:(h + 1) * DIFF_DV, :].astype(BF16)


def _inproj(x, g, w_in, wa, ba, *, tm, with_attn_layouts):
    m = x.shape[0]
    row = lambda w: pl.BlockSpec((tm, w), lambda i: (i, 0))
    full = lambda a: pl.BlockSpec(a.shape, lambda i: (0,) * a.ndim, pipeline_mode=pl.Buffered(1))
    out_shape = [jax.ShapeDtypeStruct((m, GLA_QK), F32), jax.ShapeDtypeStruct((m, GLA_QK), F32),
                 jax.ShapeDtypeStruct((m, GLA_WIDTH), F32), jax.ShapeDtypeStruct((m, GLA_WIDTH), F32),
                 jax.ShapeDtypeStruct((m, GLA_QK), F32), jax.ShapeDtypeStruct((m, DIFF_WIDTH), BF16),
                 jax.ShapeDtypeStruct((m, DIFF_WIDTH), F32), jax.ShapeDtypeStruct((m, DIFF_WIDTH), F32)]
    out_specs = [row(GLA_QK), row(GLA_QK), row(GLA_WIDTH), row(GLA_WIDTH), row(GLA_QK), row(DIFF_WIDTH),
                 row(DIFF_WIDTH), row(DIFF_WIDTH)]
    if with_attn_layouts:
        out_shape += [jax.ShapeDtypeStruct((m, DIFF_WIDTH), BF16),
                      jax.ShapeDtypeStruct((DIFF_HEADS, m // ATT_TK, DIFF_DV, ATT_TK), BF16)]
        out_specs += [row(DIFF_WIDTH),
                      pl.BlockSpec((DIFF_HEADS, tm // ATT_TK, DIFF_DV, ATT_TK), lambda i: (0, i, 0, 0))]
    return pl.pallas_call(
        _inproj_kernel,
        grid=(m // tm,),
        in_specs=[row(D_MODEL), full(g), full(w_in), full(wa), full(ba)],
        out_specs=out_specs,
        out_shape=out_shape,
        compiler_params=pltpu.CompilerParams(dimension_semantics=("arbitrary",), vmem_limit_bytes=VMEM_LIMIT),
        name="inproj",
    )(x, g, w_in, wa, ba)


def _level_ref_rows(b, h):
    if h >= 8:
        pieces = []
        for blk in range(CHUNK // (2 * h)):
            ref = blk * 2 * h + h
            pieces.append(jnp.broadcast_to(b[ref:ref + 1, :], (2 * h, b.shape[1])))
        return jnp.concatenate(pieces, axis=0)
    b3 = b.reshape(CHUNK // 8, 8, b.shape[1])
    sub = lax.broadcasted_iota(jnp.int32, b3.shape, 1)
    pick = lambda s: jnp.broadcast_to(b3[:, s:s + 1, :], b3.shape)
    if h == 4:
        out = pick(4)
    elif h == 2:
        out = jnp.where(sub < 4, pick(2), pick(6))
    else:
        out = jnp.where(sub < 2, pick(1), jnp.where(sub < 4, pick(3), jnp.where(sub < 6, pick(5), pick(7))))
    return out.reshape(b.shape)


def _gla_chunk(q, k, v, la, gg, gn, s_prev):
    row = lax.broadcasted_iota(jnp.int32, (CHUNK, CHUNK), 0)
    col = lax.broadcasted_iota(jnp.int32, (CHUNK, CHUNK), 1)
    tril = (row >= col).astype(F32)
    b = jnp.dot(tril, la, preferred_element_type=F32, precision=lax.Precision.HIGHEST)
    b_last = b[CHUNK - 1:CHUNK, :]

    r256 = lax.broadcasted_iota(jnp.int32, (CHUNK, GLA_QK), 0)
    bd_r = lax.broadcasted_iota(jnp.int32, (GLA_QK, GLA_QK), 0) // GLA_DK
    bd_c = lax.broadcasted_iota(jnp.int32, (GLA_QK, GLA_QK), 1) // GLA_DK
    bd_mask = bd_r == bd_c
    t_idx = lax.broadcasted_iota(jnp.int32, (CHUNK, GLA_QK), 0)
    s_idx = lax.broadcasted_iota(jnp.int32, (CHUNK, GLA_QK), 1) % CHUNK

    def head_scores(a_bf, b_bf):
        bbd = jnp.where(bd_mask, jnp.concatenate([b_bf] * GLA_HEADS, axis=0), jnp.zeros((), BF16))
        return _nt_dot(a_bf, bbd)

    scores = jnp.where(t_idx == s_idx, head_scores(q.astype(BF16), k.astype(BF16)), 0.0)
    h = CHUNK // 2
    while h >= 1:
        upper = (r256 % (2 * h)) >= h
        bref = _level_ref_rows(b, h)
        e = jnp.exp(jnp.where(upper, b - bref, bref - b))
        a_l = jnp.where(upper, q * e, 0.0).astype(BF16)
        b_l = jnp.where(upper, 0.0, k * e).astype(BF16)
        same_block = (t_idx // (2 * h)) == (s_idx // (2 * h))
        scores = scores + jnp.where(same_block, head_scores(a_l, b_l), 0.0)
        h //= 2

    q_int = (q * jnp.exp(b)).astype(BF16)
    lhs = jnp.concatenate([scores.astype(BF16), q_int], axis=1)
    lane_head = (lax.broadcasted_iota(jnp.int32, lhs.shape, 1) % GLA_QK) // GLA_DK
    lhs_stack = jnp.concatenate(
        [jnp.where(lane_head == hd, lhs, jnp.zeros((), BF16)) for hd in range(GLA_HEADS)], axis=0)
    v_bf = v.astype(BF16)
    v_stack = jnp.concatenate([v_bf[:, hd * GLA_DV:(hd + 1) * GLA_DV] for hd in range(GLA_HEADS)], axis=0)
    rhs = jnp.concatenate([v_stack, s_prev.astype(BF16)], axis=0)
    o_stack = jnp.dot(lhs_stack, rhs, preferred_element_type=F32)

    k_dec = k * jnp.exp(b_last - b)
    kd_pad = jnp.concatenate([k_dec, jnp.broadcast_to(jnp.exp(b_last), (CHUNK, GLA_QK))], axis=0)
    kd_t = kd_pad.T
    dec_col = kd_t[:, CHUNK:CHUNK + 1]
    s_lane = lax.broadcasted_iota(jnp.int32, kd_t.shape, 1)
    kd_t_bf = jnp.where(s_lane < CHUNK, kd_t, 0.0).astype(BF16)
    zeros_v = jnp.zeros((LANES - CHUNK, GLA_DV), BF16)
    upd = []
    for hd in range(GLA_HEADS):
        v_h = jnp.concatenate([v_bf[:, hd * GLA_DV:(hd + 1) * GLA_DV], zeros_v], axis=0)
        upd.append(jnp.dot(kd_t_bf[hd * GLA_DK:(hd + 1) * GLA_DK, :], v_h, preferred_element_type=F32))
    s_new = dec_col * s_prev + jnp.concatenate(upd, axis=0)

    outs = []
    for hd in range(GLA_HEADS):
        o_h = o_stack[hd * CHUNK:(hd + 1) * CHUNK, :]
        outs.append(_rms(o_h, gn))
    o = jnp.concatenate(outs, axis=1)
    o = o * (gg * (1.0 / (1.0 + jnp.exp(-gg))))
    return o.astype(BF16), s_new


def _gla_kernel(q_ref, k_ref, v_ref, la_ref, gg_ref, s0_ref, gn_ref, o_ref, sfin_ref, s_scr, *, chunks):
    j = pl.program_id(1)

    @pl.when(j == 0)
    def _():
        for hd in range(GLA_HEADS):
            s_scr[hd * GLA_DK:(hd + 1) * GLA_DK, :] = s0_ref[0, hd]

    gn = gn_ref[...]

    def body(c, carry):
        r0 = pl.multiple_of(c * CHUNK, CHUNK)
        rows = pl.ds(r0, CHUNK)
        o, s_new = _gla_chunk(q_ref[rows, :], k_ref[rows, :], v_ref[rows, :], la_ref[rows, :], gg_ref[rows, :],
                              gn, s_scr[...])
        o_ref[rows, :] = o
        s_scr[...] = s_new
        return carry

    lax.fori_loop(0, chunks, body, 0)

    @pl.when(j == pl.num_programs(1) - 1)
    def _():
        for hd in range(GLA_HEADS):
            sfin_ref[0, hd] = s_scr[hd * GLA_DK:(hd + 1) * GLA_DK, :]


def _gla(gq, gk, gv, la, gg, s0, gn, *, batch, seq):
    n_chunks = seq // CHUNK
    chunks = min(GLA_CHUNKS_PER_STEP, n_chunks)
    steps = n_chunks // chunks
    tm = chunks * CHUNK
    row = lambda w: pl.BlockSpec((tm, w), lambda bi, j: (bi * steps + j, 0))
    st = pl.BlockSpec((1, GLA_HEADS, GLA_DK, GLA_DV), lambda bi, j: (bi, 0, 0, 0))
    return pl.pallas_call(
        functools.partial(_gla_kernel, chunks=chunks),
        grid=(batch, steps),
        in_specs=[row(GLA_QK), row(GLA_QK), row(GLA_WIDTH), row(GLA_QK), row(GLA_WIDTH), st,
                  pl.BlockSpec((1, GLA_DV), lambda bi, j: (0, 0))],
        out_specs=[row(GLA_WIDTH), st],
        out_shape=[jax.ShapeDtypeStruct((batch * seq, GLA_WIDTH), BF16),
                   jax.ShapeDtypeStruct((batch, GLA_HEADS, GLA_DK, GLA_DV), F32)],
        scratch_shapes=[pltpu.VMEM((GLA_QK, GLA_DV), F32)],
        compiler_params=pltpu.CompilerParams(dimension_semantics=("arbitrary", "arbitrary"),
                                             vmem_limit_bytes=VMEM_LIMIT),
        name="gla",
    )(gq, gk, gv, la, gg, s0, gn)


def _lambda(dl, lam_init):
    s01 = jnp.sum(dl[0:1, :] * dl[1:2, :], axis=1, keepdims=True)
    s23 = jnp.sum(dl[2:3, :] * dl[3:4, :], axis=1, keepdims=True)
    return jnp.exp(s01) - jnp.exp(s23) + lam_init


def _split_maps(q):
    lane = lax.broadcasted_iota(jnp.int32, q.shape, 1)
    zero = jnp.zeros((), q.dtype)
    return jnp.concatenate([jnp.where(lane < DIFF_HEAD_DIM, q, zero), jnp.where(lane >= DIFF_HEAD_DIM, q, zero)],
                           axis=0)


def _attn_prompt_kernel(q_ref, k_ref, vt_ref, bias_ref, dl_ref, gn_ref, o_ref, acc_scr, *, lam_init):
    i = pl.program_id(1)
    tq = q_ref.shape[0]
    wq = _split_maps(q_ref[...])
    acc_scr[...] = jnp.zeros_like(acc_scr)

    def step(kblk, vt, bias, m, l):
        s = _nt_dot(kblk, wq)
        if bias is not None:
            s = s + jnp.concatenate([bias, bias], axis=1)
        m_new = jnp.maximum(m, jnp.max(s, axis=0, keepdims=True))
        alpha = jnp.exp(m - m_new)
        p = jnp.exp(s - m_new)
        l_new = alpha * l + jnp.sum(p, axis=0, keepdims=True)
        acc_scr[...] = acc_scr[...] * alpha + jnp.dot(vt, p.astype(BF16), preferred_element_type=F32)
        return m_new, l_new

    far_tk = FAR_BLOCKS * ATT_TK
    n_far = jnp.maximum(i - 1, 0) // FAR_BLOCKS

    def far_body(j, ml):
        k0 = pl.multiple_of(j * far_tk, far_tk)
        vt = jnp.concatenate([vt_ref[0, j * FAR_BLOCKS + r] for r in range(FAR_BLOCKS)], axis=1)
        return step(k_ref[pl.ds(k0, far_tk), :], vt, None, *ml)

    def near_body(j, ml):
        k0 = pl.multiple_of(j * ATT_TK, ATT_TK)
        return step(k_ref[pl.ds(k0, ATT_TK), :], vt_ref[0, j], bias_ref[0, i - j], *ml)

    m0 = jnp.full((1, 2 * tq), NEG_INF, F32)
    l0 = jnp.zeros((1, 2 * tq), F32)
    ml = lax.fori_loop(0, n_far, far_body, (m0, l0))
    _, l = lax.fori_loop(n_far * FAR_BLOCKS, i + 1, near_body, ml)

    lam = _lambda(dl_ref[...], lam_init)
    o = acc_scr[...] / l
    od = o[:, :tq] - lam * o[:, tq:]
    y = od * lax.rsqrt(jnp.mean(od * od, axis=0, keepdims=True) + RMS_EPS) * gn_ref[...] * (1.0 - lam_init)
    o_ref[...] = y.T.astype(BF16)


def _attn_prompt(dq, dkb, dvt, bias, dl, gn_col, *, lam_init):
    t = dq.shape[0]
    nq = t // ATT_TQ
    return pl.pallas_call(
        functools.partial(_attn_prompt_kernel, lam_init=lam_init),
        grid=(DIFF_HEADS, nq),
        in_specs=[pl.BlockSpec((ATT_TQ, DIFF_DV), lambda h, i: (i, h)),
                  pl.BlockSpec((t, DIFF_DV), lambda h, i: (0, h)),
                  pl.BlockSpec((1, t // ATT_TK, DIFF_DV, ATT_TK), lambda h, i: (h, 0, 0, 0)),
                  pl.BlockSpec((1,) + bias.shape[1:], lambda h, i: (h, 0, 0, 0)),
                  pl.BlockSpec(dl.shape, lambda h, i: (0, 0)),
                  pl.BlockSpec(gn_col.shape, lambda h, i: (0, 0))],
        out_specs=pl.BlockSpec((ATT_TQ, DIFF_DV), lambda h, i: (i, h)),
        out_shape=jax.ShapeDtypeStruct((t, DIFF_WIDTH), BF16),
        scratch_shapes=[pltpu.VMEM((DIFF_DV, 2 * ATT_TQ), F32)],
        compiler_params=pltpu.CompilerParams(dimension_semantics=("arbitrary", "arbitrary"),
                                             vmem_limit_bytes=VMEM_LIMIT),
        name="attn_prompt",
    )(dq, dkb, dvt, bias, dl, gn_col)


def _attn_sample_kernel(q_ref, kn_ref, vn_ref, kc_ref, vc_ref, bias_ref, dl_ref, gn_ref, o_ref, acc_scr, *,
                        lam_init, n_near_past):
    t = q_ref.shape[0]
    wq = _split_maps(q_ref[...])
    past = kc_ref.shape[1]
    n_far = past - n_near_past
    kc = kc_ref[0]
    vc = vc_ref[0]
    k_far = kc[:n_far, :].astype(BF16)
    v_far = vc[:n_far, :].astype(BF16)
    k_near = jnp.concatenate([kc[n_far:, :], kn_ref[...]], axis=0).astype(BF16)
    v_near = jnp.concatenate([vc[n_far:, :], vn_ref[...]], axis=0).astype(BF16)
    bias = bias_ref[0]
    s_far = _nt_dot(wq, k_far)
    s_near = _nt_dot(wq, k_near) + jnp.concatenate([bias, bias], axis=0)
    m = jnp.maximum(jnp.max(s_far, axis=1, keepdims=True), jnp.max(s_near, axis=1, keepdims=True))
    p_far = jnp.exp(s_far - m)
    p_near = jnp.exp(s_near - m)
    l = jnp.sum(p_far, axis=1, keepdims=True) + jnp.sum(p_near, axis=1, keepdims=True)
    acc_scr[...] = jnp.dot(p_far.astype(BF16), v_far, preferred_element_type=F32)
    acc_scr[...] += jnp.dot(p_near.astype(BF16), v_near, preferred_element_type=F32)
    o = acc_scr[...] / l
    lam = _lambda(dl_ref[...], lam_init)
    od = o[:t, :] - lam * o[t:, :]
    o_ref[...] = (_rms(od, gn_ref[...]) * (1.0 - lam_init)).astype(BF16)


def _attn_sample(dq, dk, dv, cache_k, cache_v, bias, dl, gn_row, *, lam_init, n_near_past):
    batch, past, _ = cache_k.shape
    t = dq.shape[0] // batch
    blk = lambda: pl.BlockSpec((t, DIFF_DV), lambda b, h: (b, h))
    cache = lambda: pl.BlockSpec((1, past, DIFF_DV), lambda b, h: (b, 0, h))
    return pl.pallas_call(
        functools.partial(_attn_sample_kernel, lam_init=lam_init, n_near_past=n_near_past),
        grid=(batch, DIFF_HEADS),
        in_specs=[blk(), blk(), blk(), cache(), cache(),
                  pl.BlockSpec((1,) + bias.shape[1:], lambda b, h: (h, 0, 0)),
                  pl.BlockSpec(dl.shape, lambda b, h: (0, 0)),
                  pl.BlockSpec(gn_row.shape, lambda b, h: (0, 0))],
        out_specs=blk(),
        out_shape=jax.ShapeDtypeStruct((batch * t, DIFF_WIDTH), BF16),
        scratch_shapes=[pltpu.VMEM((2 * t, DIFF_DV), F32)],
        compiler_params=pltpu.CompilerParams(dimension_semantics=("arbitrary", "arbitrary"),
                                             vmem_limit_bytes=VMEM_LIMIT),
        name="attn_sample",
    )(dq, dk, dv, cache_k, cache_v, bias, dl, gn_row)


def _mix_ffn_kernel(x_ref, og_ref, od_ref, wo_ref, g_ref, wi_ref, wf_ref, y_ref):
    x = (x_ref[...]
         + jnp.dot(og_ref[...], wo_ref[:GLA_WIDTH, :], preferred_element_type=F32)
         + jnp.dot(od_ref[...], wo_ref[GLA_WIDTH:, :], preferred_element_type=F32))
    hn = _rms(x, g_ref[...]).astype(BF16)
    y_ref[...] = x
    for c0 in range(0, D_FF, FF_CHUNK):
        c1 = min(c0 + FF_CHUNK, D_FF)
        gate = jnp.dot(hn, wi_ref[:, c0:c1], preferred_element_type=F32)
        up = jnp.dot(hn, wi_ref[:, D_FF + c0:D_FF + c1], preferred_element_type=F32)
        hmid = (gate * (1.0 / (1.0 + jnp.exp(-gate))) * up).astype(BF16)
        y_ref[...] += jnp.dot(hmid, wf_ref[c0:c1, :], preferred_element_type=F32)


def _mix_ffn(x, og, od, w_out, g, w_ffn_in, w_ffn_out, *, tm):
    m = x.shape[0]
    row = lambda w: pl.BlockSpec((tm, w), lambda i: (i, 0))
    full = lambda a: pl.BlockSpec(a.shape, lambda i: (0,) * a.ndim, pipeline_mode=pl.Buffered(1))
    return pl.pallas_call(
        _mix_ffn_kernel,
        grid=(m // tm,),
        in_specs=[row(D_MODEL), row(GLA_WIDTH), row(DIFF_WIDTH), full(w_out), full(g), full(w_ffn_in),
                  full(w_ffn_out)],
        out_specs=row(D_MODEL),
        out_shape=jax.ShapeDtypeStruct((m, D_MODEL), F32),
        compiler_params=pltpu.CompilerParams(dimension_semantics=("arbitrary",), vmem_limit_bytes=VMEM_LIMIT),
        name="mix_ffn",
    )(x, og, od, w_out, g, w_ffn_in, w_ffn_out)


def _final_norm_kernel(x_ref, g_ref, y_ref):
    y_ref[...] = _rms(x_ref[...], g_ref[...])


def _final_norm(x, g, *, tm):
    m = x.shape[0]
    return pl.pallas_call(
        _final_norm_kernel,
        grid=(m // tm,),
        in_specs=[pl.BlockSpec((tm, D_MODEL), lambda i: (i, 0)), pl.BlockSpec(g.shape, lambda i: (0, 0))],
        out_specs=pl.BlockSpec((tm, D_MODEL), lambda i: (i, 0)),
        out_shape=jax.ShapeDtypeStruct((m, D_MODEL), F32),
        compiler_params=pltpu.CompilerParams(dimension_semantics=("arbitrary",), vmem_limit_bytes=VMEM_LIMIT),
        name="final_norm",
    )(x, g)


def _t5_bucket(rel):
    nb = T5_BUCKETS // 2
    max_exact = nb // 2
    ret = jnp.where(rel > 0, nb, 0)
    n = jnp.abs(rel)
    nf = jnp.maximum(n, 1).astype(F32)
    large = max_exact + (jnp.log(nf / max_exact) / math.log(T5_MAX_DIST / max_exact)
                         * (nb - max_exact)).astype(jnp.int32)
    large = jnp.minimum(large, nb - 1)
    return ret + jnp.where(n < max_exact, n, large)


def _bias_minus_far(rel_bias, rel):
    far = rel_bias[_t5_bucket(jnp.full((), -T5_MAX_DIST, jnp.int32))]
    return jnp.moveaxis(rel_bias[_t5_bucket(rel)] - far, -1, 0).astype(F32)


def _prompt_bias_tiles(rel_bias):
    s = jnp.arange(ATT_TK, dtype=jnp.int32)[:, None]
    t = jnp.arange(ATT_TQ, dtype=jnp.int32)[None, :]
    tiles = []
    for d in range(3):
        rel = s - t - d * ATT_TK
        bias = _bias_minus_far(rel_bias, rel)
        if d == 0:
            bias = jnp.where((s // CHUNK) <= (t // CHUNK), bias, NEG_INF)
        tiles.append(bias)
    return jnp.stack(tiles, axis=1)


def _sample_bias_tile(rel_bias, t_new, n_near_past):
    q = jnp.arange(t_new, dtype=jnp.int32)[:, None]
    k = jnp.arange(n_near_past + t_new, dtype=jnp.int32)[None, :] - n_near_past
    return _bias_minus_far(rel_bias, k - q)


def _pack_w_in(w_in):
    c_glr = _COLS["glr"][0]
    pad = jnp.zeros((w_in.shape[0], GATE_PAD - GLA_GATE_RANK), w_in.dtype)
    return jnp.concatenate([w_in[:, :c_glr + GLA_GATE_RANK], pad, w_in[:, c_glr + GLA_GATE_RANK:]],
                           axis=1).astype(BF16)


def _trunk(x, cache_k, cache_v, gla_state, params, *, prompt):
    batch, seq, _ = x.shape
    m = batch * seq
    x = x.reshape(m, D_MODEL)
    depth = params["w_in"].shape[0]
    tm_proj = min(512, m)
    tm_ffn = min(512, m)
    new_k, new_v, new_s = [], [], []
    if prompt:
        bias = _prompt_bias_tiles(params["rel_bias"])
    else:
        past = cache_k.shape[2]
        n_near_past = LANES
        bias = _sample_bias_tile(params["rel_bias"], seq, n_near_past)
    for l in range(depth):
        lam_init = 0.8 - 0.6 * math.exp(-0.3 * l)
        w_in = _pack_w_in(params["w_in"][l])
        wa = jnp.concatenate([params["gla_w_alpha2"][l],
                              jnp.zeros((GATE_PAD - GLA_GATE_RANK, GLA_QK), F32)], axis=0).astype(BF16)
        ba = params["gla_b_alpha"][l].reshape(1, GLA_QK)
        outs = _inproj(x, params["norm_mix_g"][l].reshape(1, D_MODEL), w_in, wa, ba,
                       tm=tm_proj, with_attn_layouts=prompt)
        gq, gk, gv, gg, la, dq, dk, dv = outs[:8]
        s0 = jnp.zeros((batch, GLA_HEADS, GLA_DK, GLA_DV), F32) if gla_state is None else gla_state[l]
        o_gla, s_fin = _gla(gq, gk, gv, la, gg, s0, params["gla_norm_g"][l].reshape(1, GLA_DV),
                            batch=batch, seq=seq)
        dl = params["diff_lambda"][l]
        if prompt:
            dkb, dvt = outs[8:]
            o_diff = _attn_prompt(dq, dkb, dvt, bias, dl, params["diff_norm_g"][l].reshape(DIFF_DV, 1),
                                  lam_init=lam_init)
        else:
            o_diff = _attn_sample(dq, dk, dv, cache_k[l].reshape(batch, past, DIFF_WIDTH),
                                  cache_v[l].reshape(batch, past, DIFF_WIDTH), bias, dl,
                                  params["diff_norm_g"][l].reshape(1, DIFF_DV),
                                  lam_init=lam_init, n_near_past=n_near_past)
        x = _mix_ffn(x, o_gla, o_diff, params["w_out"][l].astype(BF16),
                     params["norm_ffn_g"][l].reshape(1, D_MODEL), params["w_ffn_in"][l].astype(BF16),
                     params["w_ffn_out"][l].astype(BF16), tm=tm_ffn)
        new_k.append(dk.reshape(batch, seq, DIFF_HEADS, DIFF_DV))
        new_v.append(dv.reshape(batch, seq, DIFF_HEADS, DIFF_DV))
        new_s.append(s_fin)
    y = _final_norm(x, params["final_norm_g"].reshape(1, D_MODEL), tm=tm_ffn).reshape(batch, seq, D_MODEL)
    return y, jnp.stack(new_k), jnp.stack(new_v), jnp.stack(new_s)


def kernel(x_prompt, x_sample, cache_diff_k, cache_diff_v, state_gla, w_in, gla_w_alpha2, gla_b_alpha, gla_norm_g,
           diff_lambda, diff_norm_g, w_out, norm_mix_g, norm_ffn_g, w_ffn_in, w_ffn_out, final_norm_g, rel_bias):
    params = dict(w_in=w_in, gla_w_alpha2=gla_w_alpha2, gla_b_alpha=gla_b_alpha, gla_norm_g=gla_norm_g,
                  diff_lambda=diff_lambda, diff_norm_g=diff_norm_g, w_out=w_out, norm_mix_g=norm_mix_g,
                  norm_ffn_g=norm_ffn_g, w_ffn_in=w_ffn_in, w_ffn_out=w_ffn_out, final_norm_g=final_norm_g,
                  rel_bias=rel_bias)
    y_p, k_p, v_p, s_p = _trunk(x_prompt, None, None, None, params, prompt=True)
    y_s, k_s, v_s, s_s = _trunk(x_sample, cache_diff_k, cache_diff_v, state_gla, params, prompt=False)
    return (y_p, y_s, k_p, v_p, s_p, k_s, v_s, s_s)
```

```python
import functools
import math

import jax
import jax.numpy as jnp
from jax import lax
from jax.experimental import pallas as pl
from jax.experimental.pallas import tpu as pltpu

F32 = jnp.float32
BF16 = jnp.bfloat16

D_MODEL = 1024
CHUNK = 64
GLA_HEADS = 4
GLA_DK = 64
GLA_DV = 128
GLA_WIDTH = GLA_HEADS * GLA_DV
GLA_QK = GLA_HEADS * GLA_DK
GLA_GATE_RANK = 16
GLA_GATE_NORMALIZER = 16.0
DIFF_HEADS = 4
DIFF_DV = 128
DIFF_HEAD_DIM = 64
DIFF_WIDTH = DIFF_HEADS * DIFF_DV
T5_BUCKETS = 32
T5_MAX_DIST = 128
D_FF = 2816
NEG_INF = -1e30
RMS_EPS = 1e-6

LANES = 128
GATE_PAD = LANES
VMEM_LIMIT = 56 * 1024 * 1024

_COLS = {}
_off = 0
for _name, _w in (("gq", GLA_QK), ("gk", GLA_QK), ("gv", GLA_WIDTH), ("gg", GLA_WIDTH), ("glr", GATE_PAD),
                  ("dq", DIFF_WIDTH), ("dk", DIFF_WIDTH), ("dv", DIFF_WIDTH)):
    _COLS[_name] = (_off, _w)
    _off += _w
D_IN_PAD = _off

ATT_TQ = 256
ATT_TK = 512
BIAS_TILES = 4
LOG2E = math.log2(math.e)
GLA_CHUNKS_PER_STEP = 8
FF_CHUNK = 512


def _nt_dot(a, b):
    return lax.dot_general(a, b, (((1,), (1,)), ((), ())), preferred_element_type=F32)


def _rms(x, g):
    return x * lax.rsqrt(jnp.mean(x * x, axis=-1, keepdims=True) + RMS_EPS) * g


def _inproj_kernel(x_ref, g_ref, w_ref, wa_ref, ba_ref,
                   gq_ref, gk_ref, gv_ref, gg_ref, la_ref, dq_ref, dk_ref, dv_ref, *attn_refs):
    xn = _rms(x_ref[...], g_ref[...]).astype(BF16)

    def proj(name):
        c0, w = _COLS[name]
        return jnp.dot(xn, w_ref[:, c0:c0 + w], preferred_element_type=F32)

    gq_ref[...] = proj("gq") * (GLA_DK ** -0.5)
    gk_ref[...] = proj("gk")
    gv_ref[...] = proj("gv")
    gg_ref[...] = proj("gg")
    z = jnp.dot(proj("glr").astype(BF16), wa_ref[...], preferred_element_type=F32) + ba_ref[...]
    la_ref[...] = (jnp.minimum(z, 0.0) - jnp.log1p(jnp.exp(-jnp.abs(z)))) / GLA_GATE_NORMALIZER
    dq_ref[...] = (proj("dq") * (DIFF_HEAD_DIM ** -0.5 * LOG2E)).astype(BF16)
    dk = proj("dk")
    dv = proj("dv")
    tm = dk.shape[0]
    for h in range(DIFF_HEADS):
        dk_ref[pl.ds(h, tm, stride=DIFF_HEADS), :] = dk[:, h * DIFF_DV:(h + 1) * DIFF_DV]
        dv_ref[pl.ds(h, tm, stride=DIFF_HEADS), :] = dv[:, h * DIFF_DV:(h + 1) * DIFF_DV]
    if attn_refs:
        dkb_ref, dvt_ref = attn_refs
        dkb_ref[...] = dk.astype(BF16)
        for r in range(tm // ATT_TK):
            vt = dv[r * ATT_TK:(r + 1) * ATT_TK, :].T
            for h in range(DIFF_HEADS):
                dvt_ref[h, r] = vt[h * DIFF_DV:(h + 1) * DIFF_DV, :].astype(BF16)


def _inproj(x, g, w_in, wa, ba, *, tm, with_attn_layouts):
    m = x.shape[0]
    row = lambda w: pl.BlockSpec((tm, w), lambda i: (i, 0))
    full = lambda a: pl.BlockSpec(a.shape, lambda i: (0,) * a.ndim, pipeline_mode=pl.Buffered(1))
    out_shape = [jax.ShapeDtypeStruct((m, GLA_QK), F32), jax.ShapeDtypeStruct((m, GLA_QK), F32),
                 jax.ShapeDtypeStruct((m, GLA_WIDTH), F32), jax.ShapeDtypeStruct((m, GLA_WIDTH), F32),
                 jax.ShapeDtypeStruct((m, GLA_QK), F32), jax.ShapeDtypeStruct((m, DIFF_WIDTH), BF16),
                 jax.ShapeDtypeStruct((m * DIFF_HEADS, DIFF_DV), F32),
                 jax.ShapeDtypeStruct((m * DIFF_HEADS, DIFF_DV), F32)]
    heads_rows = pl.BlockSpec((tm * DIFF_HEADS, DIFF_DV), lambda i: (i, 0))
    out_specs = [row(GLA_QK), row(GLA_QK), row(GLA_WIDTH), row(GLA_WIDTH), row(GLA_QK), row(DIFF_WIDTH),
                 heads_rows, heads_rows]
    if with_attn_layouts:
        out_shape += [jax.ShapeDtypeStruct((m, DIFF_WIDTH), BF16),
                      jax.ShapeDtypeStruct((DIFF_HEADS, m // ATT_TK, DIFF_DV, ATT_TK), BF16)]
        out_specs += [row(DIFF_WIDTH),
                      pl.BlockSpec((DIFF_HEADS, tm // ATT_TK, DIFF_DV, ATT_TK), lambda i: (0, i, 0, 0))]
    return pl.pallas_call(
        _inproj_kernel,
        grid=(m // tm,),
        in_specs=[row(D_MODEL), full(g), full(w_in), full(wa), full(ba)],
        out_specs=out_specs,
        out_shape=out_shape,
        compiler_params=pltpu.CompilerParams(dimension_semantics=("arbitrary",), vmem_limit_bytes=VMEM_LIMIT),
        name="inproj",
    )(x, g, w_in, wa, ba)


def _level_ref_rows(b, h):
    if h >= 8:
        pieces = []
        for blk in range(CHUNK // (2 * h)):
            ref = blk * 2 * h + h
            pieces.append(jnp.broadcast_to(b[ref:ref + 1, :], (2 * h, b.shape[1])))
        return jnp.concatenate(pieces, axis=0)
    b3 = b.reshape(CHUNK // 8, 8, b.shape[1])
    sub = lax.broadcasted_iota(jnp.int32, b3.shape, 1)
    pick = lambda s: jnp.broadcast_to(b3[:, s:s + 1, :], b3.shape)
    if h == 4:
        out = pick(4)
    elif h == 2:
        out = jnp.where(sub < 4, pick(2), pick(6))
    else:
        out = jnp.where(sub < 2, pick(1), jnp.where(sub < 4, pick(3), jnp.where(sub < 6, pick(5), pick(7))))
    return out.reshape(b.shape)


def _gla_chunk(q, k, v, la, gg, gn, s_prev):
    row = lax.broadcasted_iota(jnp.int32, (CHUNK, CHUNK), 0)
    col = lax.broadcasted_iota(jnp.int32, (CHUNK, CHUNK), 1)
    tril = (row >= col).astype(F32)
    b = jnp.dot(tril, la, preferred_element_type=F32, precision=lax.Precision.HIGHEST)
    b_last = b[CHUNK - 1:CHUNK, :]

    r256 = lax.broadcasted_iota(jnp.int32, (CHUNK, GLA_QK), 0)
    bd_r = lax.broadcasted_iota(jnp.int32, (GLA_QK, GLA_QK), 0) // GLA_DK
    bd_c = lax.broadcasted_iota(jnp.int32, (GLA_QK, GLA_QK), 1) // GLA_DK
    bd_mask = bd_r == bd_c
    t_idx = lax.broadcasted_iota(jnp.int32, (CHUNK, GLA_QK), 0)
    s_idx = lax.broadcasted_iota(jnp.int32, (CHUNK, GLA_QK), 1) % CHUNK

    def head_scores(a_bf, b_bf):
        bbd = jnp.where(bd_mask, jnp.concatenate([b_bf] * GLA_HEADS, axis=0), jnp.zeros((), BF16))
        return _nt_dot(a_bf, bbd)

    scores = jnp.where(t_idx == s_idx, head_scores(q.astype(BF16), k.astype(BF16)), 0.0)
    h = CHUNK // 2
    while h >= 1:
        upper = (r256 % (2 * h)) >= h
        bref = _level_ref_rows(b, h)
        e = jnp.exp(jnp.where(upper, b - bref, bref - b))
        a_l = jnp.where(upper, q * e, 0.0).astype(BF16)
        b_l = jnp.where(upper, 0.0, k * e).astype(BF16)
        same_block = (t_idx // (2 * h)) == (s_idx // (2 * h))
        scores = scores + jnp.where(same_block, head_scores(a_l, b_l), 0.0)
        h //= 2

    q_int = (q * jnp.exp(b)).astype(BF16)
    lhs = jnp.concatenate([scores.astype(BF16), q_int], axis=1)
    lane_head = (lax.broadcasted_iota(jnp.int32, lhs.shape, 1) % GLA_QK) // GLA_DK
    lhs_stack = jnp.concatenate(
        [jnp.where(lane_head == hd, lhs, jnp.zeros((), BF16)) for hd in range(GLA_HEADS)], axis=0)
    v_bf = v.astype(BF16)
    v_stack = jnp.concatenate([v_bf[:, hd * GLA_DV:(hd + 1) * GLA_DV] for hd in range(GLA_HEADS)], axis=0)
    rhs = jnp.concatenate([v_stack, s_prev.astype(BF16)], axis=0)
    o_stack = jnp.dot(lhs_stack, rhs, preferred_element_type=F32)

    k_dec = k * jnp.exp(b_last - b)
    kd_pad = jnp.concatenate([k_dec, jnp.broadcast_to(jnp.exp(b_last), (CHUNK, GLA_QK))], axis=0)
    kd_t = kd_pad.T
    dec_col = kd_t[:, CHUNK:CHUNK + 1]
    s_lane = lax.broadcasted_iota(jnp.int32, kd_t.shape, 1)
    kd_t_bf = jnp.where(s_lane < CHUNK, kd_t, 0.0).astype(BF16)
    zeros_v = jnp.zeros((LANES - CHUNK, GLA_DV), BF16)
    upd = []
    for hd in range(GLA_HEADS):
        v_h = jnp.concatenate([v_bf[:, hd * GLA_DV:(hd + 1) * GLA_DV], zeros_v], axis=0)
        upd.append(jnp.dot(kd_t_bf[hd * GLA_DK:(hd + 1) * GLA_DK, :], v_h, preferred_element_type=F32))
    s_new = dec_col * s_prev + jnp.concatenate(upd, axis=0)

    outs = []
    for hd in range(GLA_HEADS):
        o_h = o_stack[hd * CHUNK:(hd + 1) * CHUNK, :]
        outs.append(_rms(o_h, gn))
    o = jnp.concatenate(outs, axis=1)
    o = o * (gg * (1.0 / (1.0 + jnp.exp(-gg))))
    return o.astype(BF16), s_new


def _gla_kernel(q_ref, k_ref, v_ref, la_ref, gg_ref, s0_ref, gn_ref, o_ref, sfin_ref, s_scr, *, chunks):
    j = pl.program_id(1)

    @pl.when(j == 0)
    def _():
        for hd in range(GLA_HEADS):
            s_scr[hd * GLA_DK:(hd + 1) * GLA_DK, :] = s0_ref[0, hd]

    gn = gn_ref[...]

    def body(c, carry):
        r0 = pl.multiple_of(c * CHUNK, CHUNK)
        rows = pl.ds(r0, CHUNK)
        o, s_new = _gla_chunk(q_ref[rows, :], k_ref[rows, :], v_ref[rows, :], la_ref[rows, :], gg_ref[rows, :],
                              gn, s_scr[...])
        o_ref[rows, :] = o
        s_scr[...] = s_new
        return carry

    lax.fori_loop(0, chunks, body, 0)

    @pl.when(j == pl.num_programs(1) - 1)
    def _():
        for hd in range(GLA_HEADS):
            sfin_ref[0, hd] = s_scr[hd * GLA_DK:(hd + 1) * GLA_DK, :]


def _gla(gq, gk, gv, la, gg, s0, gn, *, batch, seq):
    n_chunks = seq // CHUNK
    chunks = min(GLA_CHUNKS_PER_STEP, n_chunks)
    steps = n_chunks // chunks
    tm = chunks * CHUNK
    row = lambda w: pl.BlockSpec((tm, w), lambda bi, j: (bi * steps + j, 0))
    st = pl.BlockSpec((1, GLA_HEADS, GLA_DK, GLA_DV), lambda bi, j: (bi, 0, 0, 0))
    return pl.pallas_call(
        functools.partial(_gla_kernel, chunks=chunks),
        grid=(batch, steps),
        in_specs=[row(GLA_QK), row(GLA_QK), row(GLA_WIDTH), row(GLA_QK), row(GLA_WIDTH), st,
                  pl.BlockSpec((1, GLA_DV), lambda bi, j: (0, 0))],
        out_specs=[row(GLA_WIDTH), st],
        out_shape=[jax.ShapeDtypeStruct((batch * seq, GLA_WIDTH), BF16),
                   jax.ShapeDtypeStruct((batch, GLA_HEADS, GLA_DK, GLA_DV), F32)],
        scratch_shapes=[pltpu.VMEM((GLA_QK, GLA_DV), F32)],
        compiler_params=pltpu.CompilerParams(dimension_semantics=("arbitrary", "arbitrary"),
                                             vmem_limit_bytes=VMEM_LIMIT),
        name="gla",
    )(gq, gk, gv, la, gg, s0, gn)


def _lambda(dl, lam_init):
    s01 = jnp.sum(dl[0:1, :] * dl[1:2, :], axis=1, keepdims=True)
    s23 = jnp.sum(dl[2:3, :] * dl[3:4, :], axis=1, keepdims=True)
    return jnp.exp(s01) - jnp.exp(s23) + lam_init


def _split_maps(q):
    lane = lax.broadcasted_iota(jnp.int32, q.shape, 1)
    zero = jnp.zeros((), q.dtype)
    return jnp.concatenate([jnp.where(lane < DIFF_HEAD_DIM, q, zero), jnp.where(lane >= DIFF_HEAD_DIM, q, zero)],
                           axis=0)


def _attn_prompt_kernel(q_ref, k_ref, vt_ref, bias_ref, dl_ref, gn_ref, o_ref, s_a, s_b, p_a, p_b, acc_scr, *,
                        lam_init):
    i = pl.program_id(1)
    n_steps = i // 2 + 1
    n_pairs = (n_steps + 1) // 2
    qt = q_ref[...].astype(F32).T
    row = lax.broadcasted_iota(jnp.int32, qt.shape, 0)
    wq = jnp.concatenate([jnp.where(row < DIFF_HEAD_DIM, qt, 0.0), jnp.where(row >= DIFF_HEAD_DIM, qt, 0.0)],
                         axis=1).astype(BF16)
    acc_scr[...] = jnp.zeros_like(acc_scr)
    p_b[...] = jnp.zeros_like(p_b)

    def qk(t):
        tc = jnp.minimum(t, n_steps - 1)
        k0 = pl.multiple_of(tc * ATT_TK, ATT_TK)
        tile = jnp.where(t < n_steps, jnp.minimum(i - 2 * tc, BIAS_TILES - 1), BIAS_TILES)
        bias = bias_ref[0, tile]
        s = (jnp.dot(k_ref[pl.ds(k0, ATT_TK), :], wq, preferred_element_type=F32)
             + jnp.concatenate([bias, bias], axis=1))
        return s, jnp.max(s, axis=0, keepdims=True)

    def softmax(s_scr, p_scr, m, l, mb):
        m_new = jnp.maximum(m, mb)
        alpha = jnp.exp2(m - m_new)
        p = jnp.exp2(s_scr[...] - m_new)
        p_scr[...] = p.astype(BF16)
        return m_new, alpha * l + jnp.sum(p, axis=0, keepdims=True), alpha

    def pv(t, p_scr, alpha):
        tc = jnp.clip(t, 0, n_steps - 1)
        acc_scr[...] = acc_scr[...] * alpha + jnp.dot(vt_ref[0, tc], p_scr[...], preferred_element_type=F32)

    def body(jj, carry):
        m, l, mb_a, alpha_b = carry
        t = 2 * jj
        s, mb_b = qk(t + 1)
        s_b[...] = s
        m, l, alpha_a = softmax(s_a, p_a, m, l, mb_a)
        pv(t - 1, p_b, alpha_b)
        s, mb_a = qk(t + 2)
        s_a[...] = s
        m, l, alpha_b = softmax(s_b, p_b, m, l, mb_b)
        pv(t, p_a, alpha_a)
        return m, l, mb_a, alpha_b

    s, mb0 = qk(0)
    s_a[...] = s
    m0 = jnp.full((1, 2 * ATT_TQ), NEG_INF, F32)
    l0 = jnp.zeros((1, 2 * ATT_TQ), F32)
    _, l, _, alpha_b = lax.fori_loop(0, n_pairs, body, (m0, l0, mb0, jnp.ones((1, 2 * ATT_TQ), F32)))
    pv(2 * n_pairs - 1, p_b, alpha_b)

    lam = _lambda(dl_ref[...], lam_init)
    o = acc_scr[...] / l
    od = o[:, :ATT_TQ] - lam * o[:, ATT_TQ:]
    y = od * lax.rsqrt(jnp.mean(od * od, axis=0, keepdims=True) + RMS_EPS) * gn_ref[...] * (1.0 - lam_init)
    o_ref[...] = y.T.astype(BF16)


def _attn_prompt(dq, dkb, dvt, bias, dl, gn_col, *, lam_init):
    t = dq.shape[0]
    assert t % ATT_TK == 0 and ATT_TK == 2 * ATT_TQ
    return pl.pallas_call(
        functools.partial(_attn_prompt_kernel, lam_init=lam_init),
        grid=(DIFF_HEADS, t // ATT_TQ),
        in_specs=[pl.BlockSpec((ATT_TQ, DIFF_DV), lambda h, i: (i, h)),
                  pl.BlockSpec((t, DIFF_DV), lambda h, i: (0, h)),
                  pl.BlockSpec((1, t // ATT_TK, DIFF_DV, ATT_TK), lambda h, i: (h, 0, 0, 0)),
                  pl.BlockSpec((1,) + bias.shape[1:], lambda h, i: (h, 0, 0, 0)),
                  pl.BlockSpec(dl.shape, lambda h, i: (0, 0)),
                  pl.BlockSpec(gn_col.shape, lambda h, i: (0, 0))],
        out_specs=pl.BlockSpec((ATT_TQ, DIFF_DV), lambda h, i: (i, h)),
        out_shape=jax.ShapeDtypeStruct((t, DIFF_WIDTH), BF16),
        scratch_shapes=[pltpu.VMEM((ATT_TK, 2 * ATT_TQ), F32), pltpu.VMEM((ATT_TK, 2 * ATT_TQ), F32),
                        pltpu.VMEM((ATT_TK, 2 * ATT_TQ), BF16), pltpu.VMEM((ATT_TK, 2 * ATT_TQ), BF16),
                        pltpu.VMEM((DIFF_DV, 2 * ATT_TQ), F32)],
        compiler_params=pltpu.CompilerParams(dimension_semantics=("arbitrary", "arbitrary"),
                                             vmem_limit_bytes=VMEM_LIMIT),
        name="attn_prompt",
    )(dq, dkb, dvt, bias, dl, gn_col)


def _attn_sample_kernel(q_ref, kn_ref, vn_ref, kc_ref, vc_ref, bias_ref, dl_ref, gn_ref, o_ref, acc_scr, *,
                        lam_init, n_near_past):
    t = q_ref.shape[0]
    past = kc_ref.shape[0] // DIFF_HEADS
    n_far = past - n_near_past
    lam = _lambda(dl_ref[...], lam_init)
    for h in range(DIFF_HEADS):
        head_rows = lambda ref, first, n: ref[pl.ds(first * DIFF_HEADS + h, n, stride=DIFF_HEADS), :]
        wq = _split_maps(q_ref[:, h * DIFF_DV:(h + 1) * DIFF_DV])
        k_far = head_rows(kc_ref, 0, n_far).astype(BF16)
        v_far = head_rows(vc_ref, 0, n_far).astype(BF16)
        k_near = jnp.concatenate([head_rows(kc_ref, n_far, n_near_past), head_rows(kn_ref, 0, t)],
                                 axis=0).astype(BF16)
        v_near = jnp.concatenate([head_rows(vc_ref, n_far, n_near_past), head_rows(vn_ref, 0, t)],
                                 axis=0).astype(BF16)
        bias = bias_ref[h]
        s_far = _nt_dot(wq, k_far)
        s_near = _nt_dot(wq, k_near) + jnp.concatenate([bias, bias], axis=0)
        m = jnp.maximum(jnp.max(s_far, axis=1, keepdims=True), jnp.max(s_near, axis=1, keepdims=True))
        p_far = jnp.exp2(s_far - m)
        p_near = jnp.exp2(s_near - m)
        l = jnp.sum(p_far, axis=1, keepdims=True) + jnp.sum(p_near, axis=1, keepdims=True)
        acc_scr[...] = jnp.dot(p_far.astype(BF16), v_far, preferred_element_type=F32)
        acc_scr[...] += jnp.dot(p_near.astype(BF16), v_near, preferred_element_type=F32)
        o = acc_scr[...] / l
        od = o[:t, :] - lam * o[t:, :]
        o_ref[:, h * DIFF_DV:(h + 1) * DIFF_DV] = (_rms(od, gn_ref[...]) * (1.0 - lam_init)).astype(BF16)


def _attn_sample(dq, dk, dv, cache_k, cache_v, bias, dl, gn_row, *, layer, batch, past, lam_init, n_near_past):
    t = dq.shape[0] // batch
    new = lambda: pl.BlockSpec((t * DIFF_HEADS, DIFF_DV), lambda b: (b, 0))
    cache = lambda: pl.BlockSpec((past * DIFF_HEADS, DIFF_DV), lambda b: (layer * batch + b, 0))
    wide = lambda: pl.BlockSpec((t, DIFF_WIDTH), lambda b: (b, 0))
    const = lambda a: pl.BlockSpec(a.shape, lambda b: (0,) * a.ndim)
    return pl.pallas_call(
        functools.partial(_attn_sample_kernel, lam_init=lam_init, n_near_past=n_near_past),
        grid=(batch,),
        in_specs=[wide(), new(), new(), cache(), cache(), const(bias), const(dl), const(gn_row)],
        out_specs=wide(),
        out_shape=jax.ShapeDtypeStruct((batch * t, DIFF_WIDTH), BF16),
        scratch_shapes=[pltpu.VMEM((2 * t, DIFF_DV), F32)],
        compiler_params=pltpu.CompilerParams(dimension_semantics=("arbitrary",), vmem_limit_bytes=VMEM_LIMIT),
        name="attn_sample",
    )(dq, dk, dv, cache_k, cache_v, bias, dl, gn_row)


def _mix_ffn_kernel(x_ref, og_ref, od_ref, wo_ref, g_ref, wi_ref, wf_ref, y_ref):
    x = (x_ref[...]
         + jnp.dot(og_ref[...], wo_ref[:GLA_WIDTH, :], preferred_element_type=F32)
         + jnp.dot(od_ref[...], wo_ref[GLA_WIDTH:, :], preferred_element_type=F32))
    hn = _rms(x, g_ref[...]).astype(BF16)
    y_ref[...] = x
    for c0 in range(0, D_FF, FF_CHUNK):
        c1 = min(c0 + FF_CHUNK, D_FF)
        gate = jnp.dot(hn, wi_ref[:, c0:c1], preferred_element_type=F32)
        up = jnp.dot(hn, wi_ref[:, D_FF + c0:D_FF + c1], preferred_element_type=F32)
        hmid = (gate * (1.0 / (1.0 + jnp.exp(-gate))) * up).astype(BF16)
        y_ref[...] += jnp.dot(hmid, wf_ref[c0:c1, :], preferred_element_type=F32)


def _mix_ffn(x, og, od, w_out, g, w_ffn_in, w_ffn_out, *, tm):
    m = x.shape[0]
    row = lambda w: pl.BlockSpec((tm, w), lambda i: (i, 0))
    full = lambda a: pl.BlockSpec(a.shape, lambda i: (0,) * a.ndim, pipeline_mode=pl.Buffered(1))
    return pl.pallas_call(
        _mix_ffn_kernel,
        grid=(m // tm,),
        in_specs=[row(D_MODEL), row(GLA_WIDTH), row(DIFF_WIDTH), full(w_out), full(g), full(w_ffn_in),
                  full(w_ffn_out)],
        out_specs=row(D_MODEL),
        out_shape=jax.ShapeDtypeStruct((m, D_MODEL), F32),
        compiler_params=pltpu.CompilerParams(dimension_semantics=("arbitrary",), vmem_limit_bytes=VMEM_LIMIT),
        name="mix_ffn",
    )(x, og, od, w_out, g, w_ffn_in, w_ffn_out)


def _final_norm_kernel(x_ref, g_ref, y_ref):
    y_ref[...] = _rms(x_ref[...], g_ref[...])


def _final_norm(x, g, *, tm):
    m = x.shape[0]
    return pl.pallas_call(
        _final_norm_kernel,
        grid=(m // tm,),
        in_specs=[pl.BlockSpec((tm, D_MODEL), lambda i: (i, 0)), pl.BlockSpec(g.shape, lambda i: (0, 0))],
        out_specs=pl.BlockSpec((tm, D_MODEL), lambda i: (i, 0)),
        out_shape=jax.ShapeDtypeStruct((m, D_MODEL), F32),
        compiler_params=pltpu.CompilerParams(dimension_semantics=("arbitrary",), vmem_limit_bytes=VMEM_LIMIT),
        name="final_norm",
    )(x, g)


def _t5_bucket(rel):
    nb = T5_BUCKETS // 2
    max_exact = nb // 2
    ret = jnp.where(rel > 0, nb, 0)
    n = jnp.abs(rel)
    nf = jnp.maximum(n, 1).astype(F32)
    large = max_exact + (jnp.log(nf / max_exact) / math.log(T5_MAX_DIST / max_exact)
                         * (nb - max_exact)).astype(jnp.int32)
    large = jnp.minimum(large, nb - 1)
    return ret + jnp.where(n < max_exact, n, large)


def _bias_minus_far(rel_bias, rel):
    table = rel_bias.astype(F32)
    table = (table - table[_t5_bucket(jnp.full((), -T5_MAX_DIST, jnp.int32))]) * LOG2E
    hit = _t5_bucket(rel)[..., None, None] == jnp.arange(T5_BUCKETS, dtype=jnp.int32)[:, None]
    return jnp.moveaxis(jnp.sum(jnp.where(hit, table, 0.0), axis=-2), -1, 0)


def _prompt_bias_tiles(rel_bias):
    s = jnp.arange(ATT_TK, dtype=jnp.int32)[:, None]
    t = jnp.arange(ATT_TQ, dtype=jnp.int32)[None, :]
    tiles = []
    for d in range(BIAS_TILES):
        rel = s - t - d * ATT_TQ
        bias = _bias_minus_far(rel_bias, rel)
        allowed = (s // CHUNK - d * (ATT_TQ // CHUNK)) <= (t // CHUNK)
        tiles.append(jnp.where(allowed, bias, NEG_INF))
    tiles.append(jnp.full_like(tiles[0], NEG_INF))
    return jnp.stack(tiles, axis=1)


def _sample_bias_tile(rel_bias, t_new, n_near_past):
    q = jnp.arange(t_new, dtype=jnp.int32)[:, None]
    k = jnp.arange(n_near_past + t_new, dtype=jnp.int32)[None, :] - n_near_past
    return _bias_minus_far(rel_bias, k - q)


def _pack_w_in(w_in):
    c_glr = _COLS["glr"][0]
    pad = jnp.zeros((w_in.shape[0], GATE_PAD - GLA_GATE_RANK), w_in.dtype)
    return jnp.concatenate([w_in[:, :c_glr + GLA_GATE_RANK], pad, w_in[:, c_glr + GLA_GATE_RANK:]],
                           axis=1).astype(BF16)


def _trunk(x, cache_k, cache_v, gla_state, params, *, prompt):
    batch, seq, _ = x.shape
    m = batch * seq
    x = x.reshape(m, D_MODEL)
    depth = params["w_in"].shape[0]
    tm_proj = min(512, m)
    tm_ffn = min(512, m)
    new_k, new_v, new_s = [], [], []
    if prompt:
        bias = _prompt_bias_tiles(params["rel_bias"])
    else:
        past = cache_k.shape[2]
        n_near_past = LANES
        bias = _sample_bias_tile(params["rel_bias"], seq, n_near_past)
        cache_k = cache_k.reshape(-1, DIFF_DV)
        cache_v = cache_v.reshape(-1, DIFF_DV)
    for l in range(depth):
        lam_init = 0.8 - 0.6 * math.exp(-0.3 * l)
        w_in = _pack_w_in(params["w_in"][l])
        wa = jnp.concatenate([params["gla_w_alpha2"][l],
                              jnp.zeros((GATE_PAD - GLA_GATE_RANK, GLA_QK), F32)], axis=0).astype(BF16)
        ba = params["gla_b_alpha"][l].reshape(1, GLA_QK)
        outs = _inproj(x, params["norm_mix_g"][l].reshape(1, D_MODEL), w_in, wa, ba,
                       tm=tm_proj, with_attn_layouts=prompt)
        gq, gk, gv, gg, la, dq, dk, dv = outs[:8]
        s0 = jnp.zeros((batch, GLA_HEADS, GLA_DK, GLA_DV), F32) if gla_state is None else gla_state[l]
        o_gla, s_fin = _gla(gq, gk, gv, la, gg, s0, params["gla_norm_g"][l].reshape(1, GLA_DV),
                            batch=batch, seq=seq)
        dl = params["diff_lambda"][l]
        if prompt:
            dkb, dvt = outs[8:]
            o_diff = _attn_prompt(dq, dkb, dvt, bias, dl, params["diff_norm_g"][l].reshape(DIFF_DV, 1),
                                  lam_init=lam_init)
        else:
            o_diff = _attn_sample(dq, dk, dv, cache_k, cache_v, bias, dl,
                                  params["diff_norm_g"][l].reshape(1, DIFF_DV), layer=l, batch=batch, past=past,
                                  lam_init=lam_init, n_near_past=n_near_past)
        x = _mix_ffn(x, o_gla, o_diff, params["w_out"][l].astype(BF16),
                     params["norm_ffn_g"][l].reshape(1, D_MODEL), params["w_ffn_in"][l].astype(BF16),
                     params["w_ffn_out"][l].astype(BF16), tm=tm_ffn)
        new_k.append(dk.reshape(batch, seq, DIFF_HEADS, DIFF_DV))
        new_v.append(dv.reshape(batch, seq, DIFF_HEADS, DIFF_DV))
        new_s.append(s_fin)
    y = _final_norm(x, params["final_norm_g"].reshape(1, D_MODEL), tm=tm_ffn).reshape(batch, seq, D_MODEL)
    return y, jnp.stack(new_k), jnp.stack(new_v), jnp.stack(new_s)


def kernel(x_prompt, x_sample, cache_diff_k, cache_diff_v, state_gla, w_in, gla_w_alpha2, gla_b_alpha, gla_norm_g,
           diff_lambda, diff_norm_g, w_out, norm_mix_g, norm_ffn_g, w_ffn_in, w_ffn_out, final_norm_g, rel_bias):
    params = dict(w_in=w_in, gla_w_alpha2=gla_w_alpha2, gla_b_alpha=gla_b_alpha, gla_norm_g=gla_norm_g,
                  diff_lambda=diff_lambda, diff_norm_g=diff_norm_g, w_out=w_out, norm_mix_g=norm_mix_g,
                  norm_ffn_g=norm_ffn_g, w_ffn_in=w_ffn_in, w_ffn_out=w_ffn_out, final_norm_g=final_norm_g,
                  rel_bias=rel_bias)
    y_p, k_p, v_p, s_p = _trunk(x_prompt, None, None, None, params, prompt=True)
    y_s, k_s, v_s, s_s = _trunk(x_sample, cache_diff_k, cache_diff_v, state_gla, params, prompt=False)
    return (y_p, y_s, k_p, v_p, s_p, k_s, v_s, s_s)
```

```python
import functools
import math

import jax
import jax.numpy as jnp
from jax import lax
from jax.experimental import pallas as pl
from jax.experimental.pallas import tpu as pltpu

F32 = jnp.float32
BF16 = jnp.bfloat16

D_MODEL = 1024
CHUNK = 64
GLA_HEADS = 4
GLA_DK = 64
GLA_DV = 128
GLA_WIDTH = GLA_HEADS * GLA_DV
GLA_QK = GLA_HEADS * GLA_DK
GLA_GATE_RANK = 16
GLA_GATE_NORMALIZER = 16.0
DIFF_HEADS = 4
DIFF_DV = 128
DIFF_HEAD_DIM = 64
DIFF_WIDTH = DIFF_HEADS * DIFF_DV
T5_BUCKETS = 32
T5_MAX_DIST = 128
D_FF = 2816
NEG_INF = -1e30
RMS_EPS = 1e-6

LANES = 128
GATE_PAD = LANES
VMEM_LIMIT = 56 * 1024 * 1024

_COLS = {}
_off = 0
for _name, _w in (("gq", GLA_QK), ("gk", GLA_QK), ("gv", GLA_WIDTH), ("gg", GLA_WIDTH), ("glr", GATE_PAD),
                  ("dq", DIFF_WIDTH), ("dk", DIFF_WIDTH), ("dv", DIFF_WIDTH)):
    _COLS[_name] = (_off, _w)
    _off += _w
D_IN_PAD = _off

ATT_TQ = 256
ATT_TK = 512
ATT_STEPS_PER_QBLOCK = ATT_TK // ATT_TQ
ATT_CHUNK = 256
ATT_VT_ROWS = DIFF_DV + 16
ATT_HEADS_PER_STEP = 2
BIAS_TILES = -(-(ATT_TK + T5_MAX_DIST) // ATT_TQ) + 1
LOG2E = math.log2(math.e)
GLA_CHUNKS_PER_STEP = 8
GLA_CHUNKS_PER_TRIP = 2
FF_CHUNK = 512


def _nt_dot(a, b):
    return lax.dot_general(a, b, (((1,), (1,)), ((), ())), preferred_element_type=F32)


def _rms(x, g):
    return x * lax.rsqrt(jnp.mean(x * x, axis=-1, keepdims=True) + RMS_EPS) * g


def _inproj_kernel(x_ref, g_ref, w_ref, wa_ref, ba_ref,
                   gq_ref, gk_ref, gv_ref, gg_ref, la_ref, dq_ref, dk_ref, dv_ref, *attn_refs):
    xn = _rms(x_ref[...], g_ref[...]).astype(BF16)

    def proj(name):
        c0, w = _COLS[name]
        return jnp.dot(xn, w_ref[:, c0:c0 + w], preferred_element_type=F32)

    gq_ref[...] = proj("gq") * (GLA_DK ** -0.5)
    gk_ref[...] = proj("gk")
    gv_ref[...] = proj("gv")
    gg_ref[...] = proj("gg")
    z = jnp.dot(proj("glr").astype(BF16), wa_ref[...], preferred_element_type=F32) + ba_ref[...]
    la_ref[...] = (jnp.minimum(z, 0.0) - jnp.log1p(jnp.exp(-jnp.abs(z)))) / GLA_GATE_NORMALIZER
    dq_ref[...] = (proj("dq") * (DIFF_HEAD_DIM ** -0.5 * LOG2E)).astype(BF16)
    dk = proj("dk")
    dv = proj("dv")
    tm = dk.shape[0]
    for h in range(DIFF_HEADS):
        dk_ref[pl.ds(h, tm, stride=DIFF_HEADS), :] = dk[:, h * DIFF_DV:(h + 1) * DIFF_DV]
        dv_ref[pl.ds(h, tm, stride=DIFF_HEADS), :] = dv[:, h * DIFF_DV:(h + 1) * DIFF_DV]
    if attn_refs:
        dkb_ref, dvt_ref = attn_refs
        dkb_ref[...] = dk.astype(BF16)
        for r in range(tm // ATT_TK):
            vt = dv[r * ATT_TK:(r + 1) * ATT_TK, :].T
            for h in range(DIFF_HEADS):
                pad_row = lax.broadcasted_iota(jnp.int32, (ATT_VT_ROWS - DIFF_DV, ATT_TK), 0)
                ones_row = jnp.where(pad_row == 0, 1.0, 0.0)
                dvt_ref[h, r] = jnp.concatenate([vt[h * DIFF_DV:(h + 1) * DIFF_DV, :], ones_row],
                                                axis=0).astype(BF16)


def _inproj(x, g, w_in, wa, ba, *, tm, with_attn_layouts):
    m = x.shape[0]
    row = lambda w: pl.BlockSpec((tm, w), lambda i: (i, 0))
    full = lambda a: pl.BlockSpec(a.shape, lambda i: (0,) * a.ndim, pipeline_mode=pl.Buffered(1))
    out_shape = [jax.ShapeDtypeStruct((m, GLA_QK), F32), jax.ShapeDtypeStruct((m, GLA_QK), F32),
                 jax.ShapeDtypeStruct((m, GLA_WIDTH), F32), jax.ShapeDtypeStruct((m, GLA_WIDTH), F32),
                 jax.ShapeDtypeStruct((m, GLA_QK), F32), jax.ShapeDtypeStruct((m, DIFF_WIDTH), BF16),
                 jax.ShapeDtypeStruct((m * DIFF_HEADS, DIFF_DV), F32),
                 jax.ShapeDtypeStruct((m * DIFF_HEADS, DIFF_DV), F32)]
    heads_rows = pl.BlockSpec((tm * DIFF_HEADS, DIFF_DV), lambda i: (i, 0))
    out_specs = [row(GLA_QK), row(GLA_QK), row(GLA_WIDTH), row(GLA_WIDTH), row(GLA_QK), row(DIFF_WIDTH),
                 heads_rows, heads_rows]
    if with_attn_layouts:
        out_shape += [jax.ShapeDtypeStruct((m, DIFF_WIDTH), BF16),
                      jax.ShapeDtypeStruct((DIFF_HEADS, m // ATT_TK, ATT_VT_ROWS, ATT_TK), BF16)]
        out_specs += [row(DIFF_WIDTH),
                      pl.BlockSpec((DIFF_HEADS, tm // ATT_TK, ATT_VT_ROWS, ATT_TK), lambda i: (0, i, 0, 0))]
    return pl.pallas_call(
        _inproj_kernel,
        grid=(m // tm,),
        in_specs=[row(D_MODEL), full(g), full(w_in), full(wa), full(ba)],
        out_specs=out_specs,
        out_shape=out_shape,
        compiler_params=pltpu.CompilerParams(dimension_semantics=("arbitrary",), vmem_limit_bytes=VMEM_LIMIT),
        name="inproj",
    )(x, g, w_in, wa, ba)


def _level_ref_rows(b, h):
    if h >= 8:
        pieces = []
        for blk in range(CHUNK // (2 * h)):
            ref = blk * 2 * h + h
            pieces.append(jnp.broadcast_to(b[ref:ref + 1, :], (2 * h, b.shape[1])))
        return jnp.concatenate(pieces, axis=0)
    b3 = b.reshape(CHUNK // 8, 8, b.shape[1])
    sub = lax.broadcasted_iota(jnp.int32, b3.shape, 1)
    pick = lambda s: jnp.broadcast_to(b3[:, s:s + 1, :], b3.shape)
    if h == 4:
        out = pick(4)
    elif h == 2:
        out = jnp.where(sub < 4, pick(2), pick(6))
    else:
        out = jnp.where(sub < 2, pick(1), jnp.where(sub < 4, pick(3), jnp.where(sub < 6, pick(5), pick(7))))
    return out.reshape(b.shape)


def _gla_chunk(q, k, v, la, gg, gn, s_prev):
    row = lax.broadcasted_iota(jnp.int32, (CHUNK, CHUNK), 0)
    col = lax.broadcasted_iota(jnp.int32, (CHUNK, CHUNK), 1)
    tril = (row >= col).astype(F32)
    b = jnp.dot(tril, la, preferred_element_type=F32, precision=lax.Precision.HIGHEST)
    b_last = b[CHUNK - 1:CHUNK, :]

    r256 = lax.broadcasted_iota(jnp.int32, (CHUNK, GLA_QK), 0)
    bd_r = lax.broadcasted_iota(jnp.int32, (GLA_QK, GLA_QK), 0) // GLA_DK
    bd_c = lax.broadcasted_iota(jnp.int32, (GLA_QK, GLA_QK), 1) // GLA_DK
    bd_mask = bd_r == bd_c
    t_idx = lax.broadcasted_iota(jnp.int32, (CHUNK, GLA_QK), 0)
    s_idx = lax.broadcasted_iota(jnp.int32, (CHUNK, GLA_QK), 1) % CHUNK

    def head_scores(a_bf, b_bf):
        bbd = jnp.where(bd_mask, jnp.concatenate([b_bf] * GLA_HEADS, axis=0), jnp.zeros((), BF16))
        return _nt_dot(a_bf, bbd)

    scores = jnp.where(t_idx == s_idx, head_scores(q.astype(BF16), k.astype(BF16)), 0.0)
    h = CHUNK // 2
    while h >= 1:
        upper = (r256 % (2 * h)) >= h
        bref = _level_ref_rows(b, h)
        e = jnp.exp(jnp.where(upper, b - bref, bref - b))
        a_l = jnp.where(upper, q * e, 0.0).astype(BF16)
        b_l = jnp.where(upper, 0.0, k * e).astype(BF16)
        same_block = (t_idx // (2 * h)) == (s_idx // (2 * h))
        scores = scores + jnp.where(same_block, head_scores(a_l, b_l), 0.0)
        h //= 2

    q_int = (q * jnp.exp(b)).astype(BF16)
    lhs = jnp.concatenate([scores.astype(BF16), q_int], axis=1)
    lane_head = (lax.broadcasted_iota(jnp.int32, lhs.shape, 1) % GLA_QK) // GLA_DK
    lhs_stack = jnp.concatenate(
        [jnp.where(lane_head == hd, lhs, jnp.zeros((), BF16)) for hd in range(GLA_HEADS)], axis=0)
    v_bf = v.astype(BF16)
    v_stack = jnp.concatenate([v_bf[:, hd * GLA_DV:(hd + 1) * GLA_DV] for hd in range(GLA_HEADS)], axis=0)
    rhs = jnp.concatenate([v_stack, s_prev.astype(BF16)], axis=0)
    o_stack = jnp.dot(lhs_stack, rhs, preferred_element_type=F32)

    k_dec = k * jnp.exp(b_last - b)
    kd_pad = jnp.concatenate([k_dec, jnp.broadcast_to(jnp.exp(b_last), (CHUNK, GLA_QK))], axis=0)
    kd_t = kd_pad.T
    dec_col = kd_t[:, CHUNK:CHUNK + 1]
    s_lane = lax.broadcasted_iota(jnp.int32, kd_t.shape, 1)
    kd_t_bf = jnp.where(s_lane < CHUNK, kd_t, 0.0).astype(BF16)
    zeros_v = jnp.zeros((LANES - CHUNK, GLA_DV), BF16)
    upd = []
    for hd in range(GLA_HEADS):
        v_h = jnp.concatenate([v_bf[:, hd * GLA_DV:(hd + 1) * GLA_DV], zeros_v], axis=0)
        upd.append(jnp.dot(kd_t_bf[hd * GLA_DK:(hd + 1) * GLA_DK, :], v_h, preferred_element_type=F32))
    s_new = dec_col * s_prev + jnp.concatenate(upd, axis=0)

    outs = []
    for hd in range(GLA_HEADS):
        o_h = o_stack[hd * CHUNK:(hd + 1) * CHUNK, :]
        outs.append(_rms(o_h, gn))
    o = jnp.concatenate(outs, axis=1)
    o = o * (gg * (1.0 / (1.0 + jnp.exp(-gg))))
    return o.astype(BF16), s_new


def _gla_kernel(q_ref, k_ref, v_ref, la_ref, gg_ref, s0_ref, gn_ref, o_ref, sfin_ref, s_scr, *, chunks):
    j = pl.program_id(1)

    @pl.when(j == 0)
    def _():
        for hd in range(GLA_HEADS):
            s_scr[hd * GLA_DK:(hd + 1) * GLA_DK, :] = s0_ref[0, hd]

    gn = gn_ref[...]

    unroll = GLA_CHUNKS_PER_TRIP if chunks % GLA_CHUNKS_PER_TRIP == 0 else 1

    def body(c, carry):
        s = s_scr[...]
        for u in range(unroll):
            r0 = pl.multiple_of((c * unroll + u) * CHUNK, CHUNK)
            rows = pl.ds(r0, CHUNK)
            o, s = _gla_chunk(q_ref[rows, :], k_ref[rows, :], v_ref[rows, :], la_ref[rows, :], gg_ref[rows, :],
                              gn, s)
            o_ref[rows, :] = o
        s_scr[...] = s
        return carry

    lax.fori_loop(0, chunks // unroll, body, 0)

    @pl.when(j == pl.num_programs(1) - 1)
    def _():
        for hd in range(GLA_HEADS):
            sfin_ref[0, hd] = s_scr[hd * GLA_DK:(hd + 1) * GLA_DK, :]


def _gla(gq, gk, gv, la, gg, s0, gn, *, batch, seq):
    n_chunks = seq // CHUNK
    chunks = min(GLA_CHUNKS_PER_STEP, n_chunks)
    steps = n_chunks // chunks
    tm = chunks * CHUNK
    row = lambda w: pl.BlockSpec((tm, w), lambda bi, j: (bi * steps + j, 0))
    st = pl.BlockSpec((1, GLA_HEADS, GLA_DK, GLA_DV), lambda bi, j: (bi, 0, 0, 0))
    return pl.pallas_call(
        functools.partial(_gla_kernel, chunks=chunks),
        grid=(batch, steps),
        in_specs=[row(GLA_QK), row(GLA_QK), row(GLA_WIDTH), row(GLA_QK), row(GLA_WIDTH), st,
                  pl.BlockSpec((1, GLA_DV), lambda bi, j: (0, 0))],
        out_specs=[row(GLA_WIDTH), st],
        out_shape=[jax.ShapeDtypeStruct((batch * seq, GLA_WIDTH), BF16),
                   jax.ShapeDtypeStruct((batch, GLA_HEADS, GLA_DK, GLA_DV), F32)],
        scratch_shapes=[pltpu.VMEM((GLA_QK, GLA_DV), F32)],
        compiler_params=pltpu.CompilerParams(dimension_semantics=("arbitrary", "arbitrary"),
                                             vmem_limit_bytes=VMEM_LIMIT),
        name="gla",
    )(gq, gk, gv, la, gg, s0, gn)


def _lambda(dl, lam_init):
    s01 = jnp.sum(dl[0:1, :] * dl[1:2, :], axis=1, keepdims=True)
    s23 = jnp.sum(dl[2:3, :] * dl[3:4, :], axis=1, keepdims=True)
    return jnp.exp(s01) - jnp.exp(s23) + lam_init


def _split_maps(q):
    lane = lax.broadcasted_iota(jnp.int32, q.shape, 1)
    zero = jnp.zeros((), q.dtype)
    return jnp.concatenate([jnp.where(lane < DIFF_HEAD_DIM, q, zero), jnp.where(lane >= DIFF_HEAD_DIM, q, zero)],
                           axis=0)


def _attn_prompt_kernel(q_ref, k_ref, vt_ref, bias_ref, dl_ref, gn_ref, o_ref, wq_scr, s_a, s_b, p_a, p_b,
                        acc_scr, *, lam_init):
    i = pl.program_id(1)
    heads = acc_scr.shape[0]
    n_steps = i // ATT_STEPS_PER_QBLOCK + 1
    n_pairs = (n_steps + 1) // 2
    for g in range(heads):
        qt = q_ref[:, g * DIFF_DV:(g + 1) * DIFF_DV].astype(F32).T
        row = lax.broadcasted_iota(jnp.int32, qt.shape, 0)
        wq_scr[g] = jnp.concatenate([jnp.where(row < DIFF_HEAD_DIM, qt, 0.0),
                                     jnp.where(row >= DIFF_HEAD_DIM, qt, 0.0)], axis=1).astype(BF16)
    acc_scr[...] = jnp.zeros_like(acc_scr)
    p_b[...] = jnp.zeros_like(p_b)

    def qk_chunk(g, t, c, s_scr):
        tc = jnp.minimum(t, n_steps - 1)
        k0 = pl.multiple_of(tc * ATT_TK + c * ATT_CHUNK, ATT_CHUNK)
        tile = jnp.where(t < n_steps, jnp.minimum(i - ATT_STEPS_PER_QBLOCK * tc, BIAS_TILES - 1), BIAS_TILES)
        rows = slice(c * ATT_CHUNK, (c + 1) * ATT_CHUNK)
        bias = bias_ref[g, tile, rows, :]
        s = (jnp.dot(k_ref[pl.ds(k0, ATT_CHUNK), g * DIFF_DV:(g + 1) * DIFF_DV], wq_scr[g],
                     preferred_element_type=F32) + jnp.concatenate([bias, bias], axis=1))
        s_scr[g, rows, :] = s
        return jnp.max(s, axis=0, keepdims=True)

    def softmax_chunk(g, c, s_scr, p_scr, m_new):
        rows = slice(c * ATT_CHUNK, (c + 1) * ATT_CHUNK)
        p_scr[g, rows, :] = jnp.exp2(s_scr[g, rows, :] - m_new).astype(BF16)

    def pv_chunk(g, t, c, p_scr, alpha):
        tc = jnp.clip(t, 0, n_steps - 1)
        cols = slice(c * ATT_CHUNK, (c + 1) * ATT_CHUNK)
        d = jnp.dot(vt_ref[g, tc, :, cols], p_scr[g, cols, :], preferred_element_type=F32)
        if c == 0:
            acc_scr[g] = acc_scr[g] * alpha + d
        else:
            acc_scr[g] += d

    def half_step(g, t, s_cur, p_cur, s_nxt, p_prev, m, mb_cur, alpha_prev):
        m_new = jnp.maximum(m, mb_cur)
        alpha = jnp.exp2(m - m_new)
        mb_nxt = None
        for c in range(ATT_TK // ATT_CHUNK):
            softmax_chunk(g, c, s_cur, p_cur, m_new)
            pv_chunk(g, t - 1, c, p_prev, alpha_prev)
            mbc = qk_chunk(g, t + 1, c, s_nxt)
            mb_nxt = mbc if mb_nxt is None else jnp.maximum(mb_nxt, mbc)
        return m_new, mb_nxt, alpha

    def body(jj, carry):
        t = 2 * jj
        out = []
        for g in range(heads):
            m, mb_a, alpha_b = carry[g]
            m, mb_b, alpha_a = half_step(g, t, s_a, p_a, s_b, p_b, m, mb_a, alpha_b)
            m, mb_a, alpha_b = half_step(g, t + 1, s_b, p_b, s_a, p_a, m, mb_b, alpha_a)
            out.append((m, mb_a, alpha_b))
        return tuple(out)

    init = []
    for g in range(heads):
        mb0 = None
        for c in range(ATT_TK // ATT_CHUNK):
            mbc = qk_chunk(g, 0, c, s_a)
            mb0 = mbc if mb0 is None else jnp.maximum(mb0, mbc)
        init.append((jnp.full((1, 2 * ATT_TQ), NEG_INF, F32), mb0, jnp.ones((1, 2 * ATT_TQ), F32)))
    final = lax.fori_loop(0, n_pairs, body, tuple(init))

    lam = _lambda(dl_ref[...], lam_init)
    for g in range(heads):
        alpha_b = final[g][2]
        for c in range(ATT_TK // ATT_CHUNK):
            pv_chunk(g, 2 * n_pairs - 1, c, p_b, alpha_b)
        acc = acc_scr[g]
        o = acc[:DIFF_DV, :] / acc[DIFF_DV:DIFF_DV + 1, :]
        od = o[:, :ATT_TQ] - lam * o[:, ATT_TQ:]
        y = od * lax.rsqrt(jnp.mean(od * od, axis=0, keepdims=True) + RMS_EPS) * gn_ref[...] * (1.0 - lam_init)
        o_ref[:, g * DIFF_DV:(g + 1) * DIFF_DV] = y.T.astype(BF16)


def _attn_prompt(dq, dkb, dvt, bias, dl, gn_col, *, lam_init):
    t = dq.shape[0]
    g = ATT_HEADS_PER_STEP
    assert t % ATT_TK == 0 and ATT_TK % ATT_TQ == 0 and DIFF_HEADS % g == 0
    tile = lambda dtype: pltpu.VMEM((g, ATT_TK, 2 * ATT_TQ), dtype)
    return pl.pallas_call(
        functools.partial(_attn_prompt_kernel, lam_init=lam_init),
        grid=(DIFF_HEADS // g, t // ATT_TQ),
        in_specs=[pl.BlockSpec((ATT_TQ, g * DIFF_DV), lambda h, i: (i, h)),
                  pl.BlockSpec((t, g * DIFF_DV), lambda h, i: (0, h)),
                  pl.BlockSpec((g, t // ATT_TK, ATT_VT_ROWS, ATT_TK), lambda h, i: (h, 0, 0, 0)),
                  pl.BlockSpec((g,) + bias.shape[1:], lambda h, i: (h, 0, 0, 0)),
                  pl.BlockSpec(dl.shape, lambda h, i: (0, 0)),
                  pl.BlockSpec(gn_col.shape, lambda h, i: (0, 0))],
        out_specs=pl.BlockSpec((ATT_TQ, g * DIFF_DV), lambda h, i: (i, h)),
        out_shape=jax.ShapeDtypeStruct((t, DIFF_WIDTH), BF16),
        scratch_shapes=[pltpu.VMEM((g, DIFF_DV, 2 * ATT_TQ), BF16), tile(F32), tile(F32), tile(BF16), tile(BF16),
                        pltpu.VMEM((g, ATT_VT_ROWS, 2 * ATT_TQ), F32)],
        compiler_params=pltpu.CompilerParams(dimension_semantics=("arbitrary", "arbitrary"),
                                             vmem_limit_bytes=VMEM_LIMIT),
        name="attn_prompt",
    )(dq, dkb, dvt, bias, dl, gn_col)


def _attn_sample_kernel(q_ref, kn_ref, vn_ref, kc_ref, vc_ref, bias_ref, dl_ref, gn_ref, o_ref, acc_scr, *,
                        lam_init, n_near_past):
    t = q_ref.shape[0]
    past = kc_ref.shape[0] // DIFF_HEADS
    n_far = past - n_near_past
    lam = _lambda(dl_ref[...], lam_init)
    for h in range(DIFF_HEADS):
        head_rows = lambda ref, first, n: ref[pl.ds(first * DIFF_HEADS + h, n, stride=DIFF_HEADS), :]
        wq = _split_maps(q_ref[:, h * DIFF_DV:(h + 1) * DIFF_DV])
        k_far = head_rows(kc_ref, 0, n_far).astype(BF16)
        v_far = head_rows(vc_ref, 0, n_far).astype(BF16)
        k_near = jnp.concatenate([head_rows(kc_ref, n_far, n_near_past), head_rows(kn_ref, 0, t)],
                                 axis=0).astype(BF16)
        v_near = jnp.concatenate([head_rows(vc_ref, n_far, n_near_past), head_rows(vn_ref, 0, t)],
                                 axis=0).astype(BF16)
        bias = bias_ref[h]
        s_far = _nt_dot(wq, k_far)
        s_near = _nt_dot(wq, k_near) + jnp.concatenate([bias, bias], axis=0)
        m = jnp.maximum(jnp.max(s_far, axis=1, keepdims=True), jnp.max(s_near, axis=1, keepdims=True))
        p_far = jnp.exp2(s_far - m)
        p_near = jnp.exp2(s_near - m)
        l = jnp.sum(p_far, axis=1, keepdims=True) + jnp.sum(p_near, axis=1, keepdims=True)
        acc_scr[...] = jnp.dot(p_far.astype(BF16), v_far, preferred_element_type=F32)
        acc_scr[...] += jnp.dot(p_near.astype(BF16), v_near, preferred_element_type=F32)
        o = acc_scr[...] / l
        od = o[:t, :] - lam * o[t:, :]
        o_ref[:, h * DIFF_DV:(h + 1) * DIFF_DV] = (_rms(od, gn_ref[...]) * (1.0 - lam_init)).astype(BF16)


def _attn_sample(dq, dk, dv, cache_k, cache_v, bias, dl, gn_row, *, layer, batch, past, lam_init, n_near_past):
    t = dq.shape[0] // batch
    new = lambda: pl.BlockSpec((t * DIFF_HEADS, DIFF_DV), lambda b: (b, 0))
    cache = lambda: pl.BlockSpec((past * DIFF_HEADS, DIFF_DV), lambda b: (layer * batch + b, 0))
    wide = lambda: pl.BlockSpec((t, DIFF_WIDTH), lambda b: (b, 0))
    const = lambda a: pl.BlockSpec(a.shape, lambda b: (0,) * a.ndim)
    return pl.pallas_call(
        functools.partial(_attn_sample_kernel, lam_init=lam_init, n_near_past=n_near_past),
        grid=(batch,),
        in_specs=[wide(), new(), new(), cache(), cache(), const(bias), const(dl), const(gn_row)],
        out_specs=wide(),
        out_shape=jax.ShapeDtypeStruct((batch * t, DIFF_WIDTH), BF16),
        scratch_shapes=[pltpu.VMEM((2 * t, DIFF_DV), F32)],
        compiler_params=pltpu.CompilerParams(dimension_semantics=("arbitrary",), vmem_limit_bytes=VMEM_LIMIT),
        name="attn_sample",
    )(dq, dk, dv, cache_k, cache_v, bias, dl, gn_row)


def _mix_ffn_kernel(x_ref, og_ref, od_ref, wo_ref, g_ref, wi_ref, wf_ref, y_ref):
    x = (x_ref[...]
         + jnp.dot(og_ref[...], wo_ref[:GLA_WIDTH, :], preferred_element_type=F32)
         + jnp.dot(od_ref[...], wo_ref[GLA_WIDTH:, :], preferred_element_type=F32))
    hn = _rms(x, g_ref[...]).astype(BF16)
    y_ref[...] = x
    for c0 in range(0, D_FF, FF_CHUNK):
        c1 = min(c0 + FF_CHUNK, D_FF)
        gate = jnp.dot(hn, wi_ref[:, c0:c1], preferred_element_type=F32)
        up = jnp.dot(hn, wi_ref[:, D_FF + c0:D_FF + c1], preferred_element_type=F32)
        hmid = (gate * (1.0 / (1.0 + jnp.exp(-gate))) * up).astype(BF16)
        y_ref[...] += jnp.dot(hmid, wf_ref[c0:c1, :], preferred_element_type=F32)


def _mix_ffn(x, og, od, w_out, g, w_ffn_in, w_ffn_out, *, tm):
    m = x.shape[0]
    row = lambda w: pl.BlockSpec((tm, w), lambda i: (i, 0))
    full = lambda a: pl.BlockSpec(a.shape, lambda i: (0,) * a.ndim, pipeline_mode=pl.Buffered(1))
    return pl.pallas_call(
        _mix_ffn_kernel,
        grid=(m // tm,),
        in_specs=[row(D_MODEL), row(GLA_WIDTH), row(DIFF_WIDTH), full(w_out), full(g), full(w_ffn_in),
                  full(w_ffn_out)],
        out_specs=row(D_MODEL),
        out_shape=jax.ShapeDtypeStruct((m, D_MODEL), F32),
        compiler_params=pltpu.CompilerParams(dimension_semantics=("arbitrary",), vmem_limit_bytes=VMEM_LIMIT),
        name="mix_ffn",
    )(x, og, od, w_out, g, w_ffn_in, w_ffn_out)


def _final_norm_kernel(x_ref, g_ref, y_ref):
    y_ref[...] = _rms(x_ref[...], g_ref[...])


def _final_norm(x, g, *, tm):
    m = x.shape[0]
    return pl.pallas_call(
        _final_norm_kernel,
        grid=(m // tm,),
        in_specs=[pl.BlockSpec((tm, D_MODEL), lambda i: (i, 0)), pl.BlockSpec(g.shape, lambda i: (0, 0))],
        out_specs=pl.BlockSpec((tm, D_MODEL), lambda i: (i, 0)),
        out_shape=jax.ShapeDtypeStruct((m, D_MODEL), F32),
        compiler_params=pltpu.CompilerParams(dimension_semantics=("arbitrary",), vmem_limit_bytes=VMEM_LIMIT),
        name="final_norm",
    )(x, g)


def _t5_bucket(rel):
    nb = T5_BUCKETS // 2
    max_exact = nb // 2
    ret = jnp.where(rel > 0, nb, 0)
    n = jnp.abs(rel)
    nf = jnp.maximum(n, 1).astype(F32)
    large = max_exact + (jnp.log(nf / max_exact) / math.log(T5_MAX_DIST / max_exact)
                         * (nb - max_exact)).astype(jnp.int32)
    large = jnp.minimum(large, nb - 1)
    return ret + jnp.where(n < max_exact, n, large)


def _bias_minus_far(rel_bias, rel):
    table = rel_bias.astype(F32)
    table = (table - table[_t5_bucket(jnp.full((), -T5_MAX_DIST, jnp.int32))]) * LOG2E
    hit = _t5_bucket(rel)[..., None, None] == jnp.arange(T5_BUCKETS, dtype=jnp.int32)[:, None]
    return jnp.moveaxis(jnp.sum(jnp.where(hit, table, 0.0), axis=-2), -1, 0)


def _prompt_bias_tiles(rel_bias):
    s = jnp.arange(ATT_TK, dtype=jnp.int32)[:, None]
    t = jnp.arange(ATT_TQ, dtype=jnp.int32)[None, :]
    tiles = []
    for d in range(BIAS_TILES):
        rel = s - t - d * ATT_TQ
        bias = _bias_minus_far(rel_bias, rel)
        allowed = (s // CHUNK - d * (ATT_TQ // CHUNK)) <= (t // CHUNK)
        tiles.append(jnp.where(allowed, bias, NEG_INF))
    tiles.append(jnp.full_like(tiles[0], NEG_INF))
    return jnp.stack(tiles, axis=1)


def _sample_bias_tile(rel_bias, t_new, n_near_past):
    q = jnp.arange(t_new, dtype=jnp.int32)[:, None]
    k = jnp.arange(n_near_past + t_new, dtype=jnp.int32)[None, :] - n_near_past
    return _bias_minus_far(rel_bias, k - q)


def _pack_w_in(w_in):
    c_glr = _COLS["glr"][0]
    pad = jnp.zeros((w_in.shape[0], GATE_PAD - GLA_GATE_RANK), w_in.dtype)
    return jnp.concatenate([w_in[:, :c_glr + GLA_GATE_RANK], pad, w_in[:, c_glr + GLA_GATE_RANK:]],
                           axis=1).astype(BF16)


def _trunk(x, cache_k, cache_v, gla_state, params, *, prompt):
    batch, seq, _ = x.shape
    m = batch * seq
    x = x.reshape(m, D_MODEL)
    depth = params["w_in"].shape[0]
    tm_proj = min(512, m)
    tm_ffn = min(512, m)
    new_k, new_v, new_s = [], [], []
    if prompt:
        bias = _prompt_bias_tiles(params["rel_bias"])
    else:
        past = cache_k.shape[2]
        n_near_past = LANES
        bias = _sample_bias_tile(params["rel_bias"], seq, n_near_past)
        cache_k = cache_k.reshape(-1, DIFF_DV)
        cache_v = cache_v.reshape(-1, DIFF_DV)
    for l in range(depth):
        lam_init = 0.8 - 0.6 * math.exp(-0.3 * l)
        w_in = _pack_w_in(params["w_in"][l])
        wa = jnp.concatenate([params["gla_w_alpha2"][l],
                              jnp.zeros((GATE_PAD - GLA_GATE_RANK, GLA_QK), F32)], axis=0).astype(BF16)
        ba = params["gla_b_alpha"][l].reshape(1, GLA_QK)
        outs = _inproj(x, params["norm_mix_g"][l].reshape(1, D_MODEL), w_in, wa, ba,
                       tm=tm_proj, with_attn_layouts=prompt)
        gq, gk, gv, gg, la, dq, dk, dv = outs[:8]
        s0 = jnp.zeros((batch, GLA_HEADS, GLA_DK, GLA_DV), F32) if gla_state is None else gla_state[l]
        o_gla, s_fin = _gla(gq, gk, gv, la, gg, s0, params["gla_norm_g"][l].reshape(1, GLA_DV),
                            batch=batch, seq=seq)
        dl = params["diff_lambda"][l]
        if prompt:
            dkb, dvt = outs[8:]
            o_diff = _attn_prompt(dq, dkb, dvt, bias, dl, params["diff_norm_g"][l].reshape(DIFF_DV, 1),
                                  lam_init=lam_init)
        else:
            o_diff = _attn_sample(dq, dk, dv, cache_k, cache_v, bias, dl,
                                  params["diff_norm_g"][l].reshape(1, DIFF_DV), layer=l, batch=batch, past=past,
                                  lam_init=lam_init, n_near_past=n_near_past)
        x = _mix_ffn(x, o_gla, o_diff, params["w_out"][l].astype(BF16),
                     params["norm_ffn_g"][l].reshape(1, D_MODEL), params["w_ffn_in"][l].astype(BF16),
                     params["w_ffn_out"][l].astype(BF16), tm=tm_ffn)
        new_k.append(dk.reshape(batch, seq, DIFF_HEADS, DIFF_DV))
        new_v.append(dv.reshape(batch, seq, DIFF_HEADS, DIFF_DV))
        new_s.append(s_fin)
    y = _final_norm(x, params["final_norm_g"].reshape(1, D_MODEL), tm=tm_ffn).reshape(batch, seq, D_MODEL)
    return y, jnp.stack(new_k), jnp.stack(new_v), jnp.stack(new_s)


def kernel(x_prompt, x_sample, cache_diff_k, cache_diff_v, state_gla, w_in, gla_w_alpha2, gla_b_alpha, gla_norm_g,
           diff_lambda, diff_norm_g, w_out, norm_mix_g, norm_ffn_g, w_ffn_in, w_ffn_out, final_norm_g, rel_bias):
    params = dict(w_in=w_in, gla_w_alpha2=gla_w_alpha2, gla_b_alpha=gla_b_alpha, gla_norm_g=gla_norm_g,
                  diff_lambda=diff_lambda, diff_norm_g=diff_norm_g, w_out=w_out, norm_mix_g=norm_mix_g,
                  norm_ffn_g=norm_ffn_g, w_ffn_in=w_ffn_in, w_ffn_out=w_ffn_out, final_norm_g=final_norm_g,
                  rel_bias=rel_bias)
    y_p, k_p, v_p, s_p = _trunk(x_prompt, None, None, None, params, prompt=True)
    y_s, k_s, v_s, s_s = _trunk(x_sample, cache_diff_k, cache_diff_v, state_gla, params, prompt=False)
    return (y_p, y_s, k_p, v_p, s_p, k_s, v_s, s_s)
```

```python
import functools
import math

import jax
import jax.numpy as jnp
from jax import lax
from jax.experimental import pallas as pl
from jax.experimental.pallas import tpu as pltpu

F32 = jnp.float32
BF16 = jnp.bfloat16

D_MODEL = 1024
CHUNK = 64
GLA_HEADS = 4
GLA_DK = 64
GLA_DV = 128
GLA_WIDTH = GLA_HEADS * GLA_DV
GLA_QK = GLA_HEADS * GLA_DK
GLA_GATE_RANK = 16
GLA_GATE_NORMALIZER = 16.0
DIFF_HEADS = 4
DIFF_DV = 128
DIFF_HEAD_DIM = 64
DIFF_WIDTH = DIFF_HEADS * DIFF_DV
T5_BUCKETS = 32
T5_MAX_DIST = 128
D_FF = 2816
NEG_INF = -1e30
RMS_EPS = 1e-6

LANES = 128
GATE_PAD = LANES
VMEM_LIMIT = 56 * 1024 * 1024

_COLS = {}
_off = 0
for _name, _w in (("gq", GLA_QK), ("gk", GLA_QK), ("gv", GLA_WIDTH), ("gg", GLA_WIDTH), ("glr", GATE_PAD),
                  ("dq", DIFF_WIDTH), ("dk", DIFF_WIDTH), ("dv", DIFF_WIDTH)):
    _COLS[_name] = (_off, _w)
    _off += _w
D_IN_PAD = _off

ATT_TQ = 256
ATT_TK = 512
ATT_STEPS_PER_QBLOCK = ATT_TK // ATT_TQ
ATT_CHUNK = 256
ATT_VT_ROWS = DIFF_DV + 16
ATT_HEADS_PER_STEP = 2
BIAS_TILES = -(-(ATT_TK + T5_MAX_DIST) // ATT_TQ) + 1
LOG2E = math.log2(math.e)
GLA_CHUNKS_PER_STEP = 8
GLA_CHUNKS_PER_TRIP = 2
FF_CHUNK = 512


def _nt_dot(a, b):
    return lax.dot_general(a, b, (((1,), (1,)), ((), ())), preferred_element_type=F32)


def _rms(x, g):
    return x * lax.rsqrt(jnp.mean(x * x, axis=-1, keepdims=True) + RMS_EPS) * g


def _inproj_kernel(x_ref, g_ref, w_ref, wa_ref, ba_ref,
                   gq_ref, gk_ref, gv_ref, gg_ref, la_ref, dq_ref, dk_ref, dv_ref, *attn_refs):
    xn = _rms(x_ref[...], g_ref[...]).astype(BF16)

    def proj(name):
        c0, w = _COLS[name]
        return jnp.dot(xn, w_ref[:, c0:c0 + w], preferred_element_type=F32)

    gq_ref[...] = proj("gq") * (GLA_DK ** -0.5)
    gk_ref[...] = proj("gk")
    gv_ref[...] = proj("gv")
    gg_ref[...] = proj("gg")
    z = jnp.dot(proj("glr").astype(BF16), wa_ref[...], preferred_element_type=F32) + ba_ref[...]
    la_ref[...] = (jnp.minimum(z, 0.0) - jnp.log1p(jnp.exp(-jnp.abs(z)))) / GLA_GATE_NORMALIZER
    dq_ref[...] = (proj("dq") * (DIFF_HEAD_DIM ** -0.5 * LOG2E)).astype(BF16)
    dk = proj("dk")
    dv = proj("dv")
    tm = dk.shape[0]
    for h in range(DIFF_HEADS):
        dk_ref[pl.ds(h, tm, stride=DIFF_HEADS), :] = dk[:, h * DIFF_DV:(h + 1) * DIFF_DV]
        dv_ref[pl.ds(h, tm, stride=DIFF_HEADS), :] = dv[:, h * DIFF_DV:(h + 1) * DIFF_DV]
    if attn_refs:
        dkb_ref, dvt_ref = attn_refs
        dkb_ref[...] = dk.astype(BF16)
        for r in range(tm // ATT_TK):
            vt = dv[r * ATT_TK:(r + 1) * ATT_TK, :].T
            for h in range(DIFF_HEADS):
                pad_row = lax.broadcasted_iota(jnp.int32, (ATT_VT_ROWS - DIFF_DV, ATT_TK), 0)
                ones_row = jnp.where(pad_row == 0, 1.0, 0.0)
                dvt_ref[h, r] = jnp.concatenate([vt[h * DIFF_DV:(h + 1) * DIFF_DV, :], ones_row],
                                                axis=0).astype(BF16)


def _inproj(x, g, w_in, wa, ba, *, layer, tm, with_attn_layouts):
    m = x.shape[0]
    row = lambda w: pl.BlockSpec((tm, w), lambda i: (i, 0))
    full = lambda a: _layer_block(a, layer)
    out_shape = [jax.ShapeDtypeStruct((m, GLA_QK), F32), jax.ShapeDtypeStruct((m, GLA_QK), F32),
                 jax.ShapeDtypeStruct((m, GLA_WIDTH), F32), jax.ShapeDtypeStruct((m, GLA_WIDTH), F32),
                 jax.ShapeDtypeStruct((m, GLA_QK), F32), jax.ShapeDtypeStruct((m, DIFF_WIDTH), BF16),
                 jax.ShapeDtypeStruct((m * DIFF_HEADS, DIFF_DV), F32),
                 jax.ShapeDtypeStruct((m * DIFF_HEADS, DIFF_DV), F32)]
    heads_rows = pl.BlockSpec((tm * DIFF_HEADS, DIFF_DV), lambda i: (i, 0))
    out_specs = [row(GLA_QK), row(GLA_QK), row(GLA_WIDTH), row(GLA_WIDTH), row(GLA_QK), row(DIFF_WIDTH),
                 heads_rows, heads_rows]
    if with_attn_layouts:
        out_shape += [jax.ShapeDtypeStruct((m, DIFF_WIDTH), BF16),
                      jax.ShapeDtypeStruct((DIFF_HEADS, m // ATT_TK, ATT_VT_ROWS, ATT_TK), BF16)]
        out_specs += [row(DIFF_WIDTH),
                      pl.BlockSpec((DIFF_HEADS, tm // ATT_TK, ATT_VT_ROWS, ATT_TK), lambda i: (0, i, 0, 0))]
    return pl.pallas_call(
        _inproj_kernel,
        grid=(m // tm,),
        in_specs=[row(D_MODEL), full(g), full(w_in), full(wa), full(ba)],
        out_specs=out_specs,
        out_shape=out_shape,
        compiler_params=pltpu.CompilerParams(dimension_semantics=("arbitrary",), vmem_limit_bytes=VMEM_LIMIT),
        name="inproj",
    )(x, g, w_in, wa, ba)


def _level_ref_rows(b, h):
    if h >= 8:
        pieces = []
        for blk in range(CHUNK // (2 * h)):
            ref = blk * 2 * h + h
            pieces.append(jnp.broadcast_to(b[ref:ref + 1, :], (2 * h, b.shape[1])))
        return jnp.concatenate(pieces, axis=0)
    b3 = b.reshape(CHUNK // 8, 8, b.shape[1])
    sub = lax.broadcasted_iota(jnp.int32, b3.shape, 1)
    pick = lambda s: jnp.broadcast_to(b3[:, s:s + 1, :], b3.shape)
    if h == 4:
        out = pick(4)
    elif h == 2:
        out = jnp.where(sub < 4, pick(2), pick(6))
    else:
        out = jnp.where(sub < 2, pick(1), jnp.where(sub < 4, pick(3), jnp.where(sub < 6, pick(5), pick(7))))
    return out.reshape(b.shape)


def _gla_chunk(q, k, v, la, gg, gn, s_prev):
    row = lax.broadcasted_iota(jnp.int32, (CHUNK, CHUNK), 0)
    col = lax.broadcasted_iota(jnp.int32, (CHUNK, CHUNK), 1)
    tril = (row >= col).astype(F32)
    b = jnp.dot(tril, la, preferred_element_type=F32, precision=lax.Precision.HIGHEST)
    b_last = b[CHUNK - 1:CHUNK, :]

    r256 = lax.broadcasted_iota(jnp.int32, (CHUNK, GLA_QK), 0)
    bd_r = lax.broadcasted_iota(jnp.int32, (GLA_QK, GLA_QK), 0) // GLA_DK
    bd_c = lax.broadcasted_iota(jnp.int32, (GLA_QK, GLA_QK), 1) // GLA_DK
    bd_mask = bd_r == bd_c
    t_idx = lax.broadcasted_iota(jnp.int32, (CHUNK, GLA_QK), 0)
    s_idx = lax.broadcasted_iota(jnp.int32, (CHUNK, GLA_QK), 1) % CHUNK

    def head_scores(a_bf, b_bf):
        bbd = jnp.where(bd_mask, jnp.concatenate([b_bf] * GLA_HEADS, axis=0), jnp.zeros((), BF16))
        return _nt_dot(a_bf, bbd)

    scores = jnp.where(t_idx == s_idx, head_scores(q.astype(BF16), k.astype(BF16)), 0.0)
    h = CHUNK // 2
    while h >= 1:
        upper = (r256 % (2 * h)) >= h
        bref = _level_ref_rows(b, h)
        e = jnp.exp(jnp.where(upper, b - bref, bref - b))
        a_l = jnp.where(upper, q * e, 0.0).astype(BF16)
        b_l = jnp.where(upper, 0.0, k * e).astype(BF16)
        same_block = (t_idx // (2 * h)) == (s_idx // (2 * h))
        scores = scores + jnp.where(same_block, head_scores(a_l, b_l), 0.0)
        h //= 2

    q_int = (q * jnp.exp(b)).astype(BF16)
    lhs = jnp.concatenate([scores.astype(BF16), q_int], axis=1)
    lane_head = (lax.broadcasted_iota(jnp.int32, lhs.shape, 1) % GLA_QK) // GLA_DK
    lhs_stack = jnp.concatenate(
        [jnp.where(lane_head == hd, lhs, jnp.zeros((), BF16)) for hd in range(GLA_HEADS)], axis=0)
    v_bf = v.astype(BF16)
    v_stack = jnp.concatenate([v_bf[:, hd * GLA_DV:(hd + 1) * GLA_DV] for hd in range(GLA_HEADS)], axis=0)
    rhs = jnp.concatenate([v_stack, s_prev.astype(BF16)], axis=0)
    o_stack = jnp.dot(lhs_stack, rhs, preferred_element_type=F32)

    k_dec = k * jnp.exp(b_last - b)
    kd_pad = jnp.concatenate([k_dec, jnp.broadcast_to(jnp.exp(b_last), (CHUNK, GLA_QK))], axis=0)
    kd_t = kd_pad.T
    dec_col = kd_t[:, CHUNK:CHUNK + 1]
    s_lane = lax.broadcasted_iota(jnp.int32, kd_t.shape, 1)
    kd_t_bf = jnp.where(s_lane < CHUNK, kd_t, 0.0).astype(BF16)
    zeros_v = jnp.zeros((LANES - CHUNK, GLA_DV), BF16)
    upd = []
    for hd in range(GLA_HEADS):
        v_h = jnp.concatenate([v_bf[:, hd * GLA_DV:(hd + 1) * GLA_DV], zeros_v], axis=0)
        upd.append(jnp.dot(kd_t_bf[hd * GLA_DK:(hd + 1) * GLA_DK, :], v_h, preferred_element_type=F32))
    s_new = dec_col * s_prev + jnp.concatenate(upd, axis=0)

    outs = []
    for hd in range(GLA_HEADS):
        o_h = o_stack[hd * CHUNK:(hd + 1) * CHUNK, :]
        outs.append(_rms(o_h, gn))
    o = jnp.concatenate(outs, axis=1)
    o = o * (gg * (1.0 / (1.0 + jnp.exp(-gg))))
    return o.astype(BF16), s_new


def _gla_kernel(q_ref, k_ref, v_ref, la_ref, gg_ref, s0_ref, gn_ref, o_ref, sfin_ref, s_scr, *, chunks):
    j = pl.program_id(1)

    @pl.when(j == 0)
    def _():
        for hd in range(GLA_HEADS):
            s_scr[hd * GLA_DK:(hd + 1) * GLA_DK, :] = s0_ref[0, hd]

    gn = gn_ref[...]

    unroll = GLA_CHUNKS_PER_TRIP if chunks % GLA_CHUNKS_PER_TRIP == 0 else 1

    def body(c, carry):
        s = s_scr[...]
        for u in range(unroll):
            r0 = pl.multiple_of((c * unroll + u) * CHUNK, CHUNK)
            rows = pl.ds(r0, CHUNK)
            o, s = _gla_chunk(q_ref[rows, :], k_ref[rows, :], v_ref[rows, :], la_ref[rows, :], gg_ref[rows, :],
                              gn, s)
            o_ref[rows, :] = o
        s_scr[...] = s
        return carry

    lax.fori_loop(0, chunks // unroll, body, 0)

    @pl.when(j == pl.num_programs(1) - 1)
    def _():
        for hd in range(GLA_HEADS):
            sfin_ref[0, hd] = s_scr[hd * GLA_DK:(hd + 1) * GLA_DK, :]


def _gla(gq, gk, gv, la, gg, s0, gn, *, batch, seq):
    n_chunks = seq // CHUNK
    chunks = min(GLA_CHUNKS_PER_STEP, n_chunks)
    steps = n_chunks // chunks
    tm = chunks * CHUNK
    row = lambda w: pl.BlockSpec((tm, w), lambda bi, j: (bi * steps + j, 0))
    st = pl.BlockSpec((1, GLA_HEADS, GLA_DK, GLA_DV), lambda bi, j: (bi, 0, 0, 0))
    return pl.pallas_call(
        functools.partial(_gla_kernel, chunks=chunks),
        grid=(batch, steps),
        in_specs=[row(GLA_QK), row(GLA_QK), row(GLA_WIDTH), row(GLA_QK), row(GLA_WIDTH), st,
                  pl.BlockSpec((1, GLA_DV), lambda bi, j: (0, 0))],
        out_specs=[row(GLA_WIDTH), st],
        out_shape=[jax.ShapeDtypeStruct((batch * seq, GLA_WIDTH), BF16),
                   jax.ShapeDtypeStruct((batch, GLA_HEADS, GLA_DK, GLA_DV), F32)],
        scratch_shapes=[pltpu.VMEM((GLA_QK, GLA_DV), F32)],
        compiler_params=pltpu.CompilerParams(dimension_semantics=("arbitrary", "arbitrary"),
                                             vmem_limit_bytes=VMEM_LIMIT),
        name="gla",
    )(gq, gk, gv, la, gg, s0, gn)


def _lambda(dl, lam_init):
    s01 = jnp.sum(dl[0:1, :] * dl[1:2, :], axis=1, keepdims=True)
    s23 = jnp.sum(dl[2:3, :] * dl[3:4, :], axis=1, keepdims=True)
    return jnp.exp(s01) - jnp.exp(s23) + lam_init


def _split_maps(q):
    lane = lax.broadcasted_iota(jnp.int32, q.shape, 1)
    zero = jnp.zeros((), q.dtype)
    return jnp.concatenate([jnp.where(lane < DIFF_HEAD_DIM, q, zero), jnp.where(lane >= DIFF_HEAD_DIM, q, zero)],
                           axis=0)


def _attn_prompt_kernel(q_ref, k_ref, vt_ref, bias_ref, dl_ref, gn_ref, o_ref, wq_scr, s_a, s_b, p_a, p_b,
                        acc_scr, *, lam_init):
    i = pl.program_id(1)
    heads = acc_scr.shape[0]
    n_steps = i // ATT_STEPS_PER_QBLOCK + 1
    n_pairs = (n_steps + 1) // 2
    for g in range(heads):
        qt = q_ref[:, g * DIFF_DV:(g + 1) * DIFF_DV].astype(F32).T
        row = lax.broadcasted_iota(jnp.int32, qt.shape, 0)
        wq_scr[g] = jnp.concatenate([jnp.where(row < DIFF_HEAD_DIM, qt, 0.0),
                                     jnp.where(row >= DIFF_HEAD_DIM, qt, 0.0)], axis=1).astype(BF16)
    acc_scr[...] = jnp.zeros_like(acc_scr)
    p_b[...] = jnp.zeros_like(p_b)

    def qk_chunk(g, t, c, s_scr):
        tc = jnp.minimum(t, n_steps - 1)
        k0 = pl.multiple_of(tc * ATT_TK + c * ATT_CHUNK, ATT_CHUNK)
        tile = jnp.where(t < n_steps, jnp.minimum(i - ATT_STEPS_PER_QBLOCK * tc, BIAS_TILES - 1), BIAS_TILES)
        rows = slice(c * ATT_CHUNK, (c + 1) * ATT_CHUNK)
        bias = bias_ref[g, tile, rows, :]
        s = (jnp.dot(k_ref[pl.ds(k0, ATT_CHUNK), g * DIFF_DV:(g + 1) * DIFF_DV], wq_scr[g],
                     preferred_element_type=F32) + jnp.concatenate([bias, bias], axis=1))
        s_scr[g, rows, :] = s
        return jnp.max(s, axis=0, keepdims=True)

    def softmax_chunk(g, c, s_scr, p_scr, m_new):
        rows = slice(c * ATT_CHUNK, (c + 1) * ATT_CHUNK)
        p_scr[g, rows, :] = jnp.exp2(s_scr[g, rows, :] - m_new).astype(BF16)

    def pv_chunk(g, t, c, p_scr, alpha):
        tc = jnp.clip(t, 0, n_steps - 1)
        cols = slice(c * ATT_CHUNK, (c + 1) * ATT_CHUNK)
        d = jnp.dot(vt_ref[g, tc, :, cols], p_scr[g, cols, :], preferred_element_type=F32)
        if c == 0:
            acc_scr[g] = acc_scr[g] * alpha + d
        else:
            acc_scr[g] += d

    def half_step(g, t, s_cur, p_cur, s_nxt, p_prev, m, mb_cur, alpha_prev):
        m_new = jnp.maximum(m, mb_cur)
        alpha = jnp.exp2(m - m_new)
        mb_nxt = None
        for c in range(ATT_TK // ATT_CHUNK):
            softmax_chunk(g, c, s_cur, p_cur, m_new)
            pv_chunk(g, t - 1, c, p_prev, alpha_prev)
            mbc = qk_chunk(g, t + 1, c, s_nxt)
            mb_nxt = mbc if mb_nxt is None else jnp.maximum(mb_nxt, mbc)
        return m_new, mb_nxt, alpha

    def body(jj, carry):
        t = 2 * jj
        out = []
        for g in range(heads):
            m, mb_a, alpha_b = carry[g]
            m, mb_b, alpha_a = half_step(g, t, s_a, p_a, s_b, p_b, m, mb_a, alpha_b)
            m, mb_a, alpha_b = half_step(g, t + 1, s_b, p_b, s_a, p_a, m, mb_b, alpha_a)
            out.append((m, mb_a, alpha_b))
        return tuple(out)

    init = []
    for g in range(heads):
        mb0 = None
        for c in range(ATT_TK // ATT_CHUNK):
            mbc = qk_chunk(g, 0, c, s_a)
            mb0 = mbc if mb0 is None else jnp.maximum(mb0, mbc)
        init.append((jnp.full((1, 2 * ATT_TQ), NEG_INF, F32), mb0, jnp.ones((1, 2 * ATT_TQ), F32)))
    final = lax.fori_loop(0, n_pairs, body, tuple(init))

    lam = _lambda(dl_ref[...], lam_init)
    for g in range(heads):
        alpha_b = final[g][2]
        for c in range(ATT_TK // ATT_CHUNK):
            pv_chunk(g, 2 * n_pairs - 1, c, p_b, alpha_b)
        acc = acc_scr[g]
        o = acc[:DIFF_DV, :] / acc[DIFF_DV:DIFF_DV + 1, :]
        od = o[:, :ATT_TQ] - lam * o[:, ATT_TQ:]
        y = od * lax.rsqrt(jnp.mean(od * od, axis=0, keepdims=True) + RMS_EPS) * gn_ref[...] * (1.0 - lam_init)
        o_ref[:, g * DIFF_DV:(g + 1) * DIFF_DV] = y.T.astype(BF16)


def _attn_prompt(dq, dkb, dvt, bias, dl, gn_col, *, lam_init):
    t = dq.shape[0]
    g = ATT_HEADS_PER_STEP
    assert t % ATT_TK == 0 and ATT_TK % ATT_TQ == 0 and DIFF_HEADS % g == 0
    tile = lambda dtype: pltpu.VMEM((g, ATT_TK, 2 * ATT_TQ), dtype)
    return pl.pallas_call(
        functools.partial(_attn_prompt_kernel, lam_init=lam_init),
        grid=(DIFF_HEADS // g, t // ATT_TQ),
        in_specs=[pl.BlockSpec((ATT_TQ, g * DIFF_DV), lambda h, i: (i, h)),
                  pl.BlockSpec((t, g * DIFF_DV), lambda h, i: (0, h)),
                  pl.BlockSpec((g, t // ATT_TK, ATT_VT_ROWS, ATT_TK), lambda h, i: (h, 0, 0, 0)),
                  pl.BlockSpec((g,) + bias.shape[1:], lambda h, i: (h, 0, 0, 0)),
                  pl.BlockSpec(dl.shape, lambda h, i: (0, 0)),
                  pl.BlockSpec(gn_col.shape, lambda h, i: (0, 0))],
        out_specs=pl.BlockSpec((ATT_TQ, g * DIFF_DV), lambda h, i: (i, h)),
        out_shape=jax.ShapeDtypeStruct((t, DIFF_WIDTH), BF16),
        scratch_shapes=[pltpu.VMEM((g, DIFF_DV, 2 * ATT_TQ), BF16), tile(F32), tile(F32), tile(BF16), tile(BF16),
                        pltpu.VMEM((g, ATT_VT_ROWS, 2 * ATT_TQ), F32)],
        compiler_params=pltpu.CompilerParams(dimension_semantics=("arbitrary", "arbitrary"),
                                             vmem_limit_bytes=VMEM_LIMIT),
        name="attn_prompt",
    )(dq, dkb, dvt, bias, dl, gn_col)


def _attn_sample_kernel(q_ref, kn_ref, vn_ref, kc_ref, vc_ref, bias_ref, dl_ref, gn_ref, o_ref, acc_scr, *,
                        lam_init, n_near_past):
    t = q_ref.shape[0]
    past = kc_ref.shape[0] // DIFF_HEADS
    n_far = past - n_near_past
    lam = _lambda(dl_ref[...], lam_init)
    for h in range(DIFF_HEADS):
        head_rows = lambda ref, first, n: ref[pl.ds(first * DIFF_HEADS + h, n, stride=DIFF_HEADS), :]
        wq = _split_maps(q_ref[:, h * DIFF_DV:(h + 1) * DIFF_DV])
        k_far = head_rows(kc_ref, 0, n_far).astype(BF16)
        v_far = head_rows(vc_ref, 0, n_far).astype(BF16)
        k_near = jnp.concatenate([head_rows(kc_ref, n_far, n_near_past), head_rows(kn_ref, 0, t)],
                                 axis=0).astype(BF16)
        v_near = jnp.concatenate([head_rows(vc_ref, n_far, n_near_past), head_rows(vn_ref, 0, t)],
                                 axis=0).astype(BF16)
        bias = bias_ref[h]
        s_far = _nt_dot(wq, k_far)
        s_near = _nt_dot(wq, k_near) + jnp.concatenate([bias, bias], axis=0)
        m = jnp.maximum(jnp.max(s_far, axis=1, keepdims=True), jnp.max(s_near, axis=1, keepdims=True))
        p_far = jnp.exp2(s_far - m)
        p_near = jnp.exp2(s_near - m)
        l = jnp.sum(p_far, axis=1, keepdims=True) + jnp.sum(p_near, axis=1, keepdims=True)
        acc_scr[...] = jnp.dot(p_far.astype(BF16), v_far, preferred_element_type=F32)
        acc_scr[...] += jnp.dot(p_near.astype(BF16), v_near, preferred_element_type=F32)
        o = acc_scr[...] / l
        od = o[:t, :] - lam * o[t:, :]
        o_ref[:, h * DIFF_DV:(h + 1) * DIFF_DV] = (_rms(od, gn_ref[...]) * (1.0 - lam_init)).astype(BF16)


def _attn_sample(dq, dk, dv, cache_k, cache_v, bias, dl, gn_row, *, layer, batch, past, lam_init, n_near_past):
    t = dq.shape[0] // batch
    new = lambda: pl.BlockSpec((t * DIFF_HEADS, DIFF_DV), lambda b: (b, 0))
    cache = lambda: pl.BlockSpec((past * DIFF_HEADS, DIFF_DV), lambda b: (layer * batch + b, 0))
    wide = lambda: pl.BlockSpec((t, DIFF_WIDTH), lambda b: (b, 0))
    const = lambda a: pl.BlockSpec(a.shape, lambda b: (0,) * a.ndim)
    return pl.pallas_call(
        functools.partial(_attn_sample_kernel, lam_init=lam_init, n_near_past=n_near_past),
        grid=(batch,),
        in_specs=[wide(), new(), new(), cache(), cache(), const(bias), const(dl), const(gn_row)],
        out_specs=wide(),
        out_shape=jax.ShapeDtypeStruct((batch * t, DIFF_WIDTH), BF16),
        scratch_shapes=[pltpu.VMEM((2 * t, DIFF_DV), F32)],
        compiler_params=pltpu.CompilerParams(dimension_semantics=("arbitrary",), vmem_limit_bytes=VMEM_LIMIT),
        name="attn_sample",
    )(dq, dk, dv, cache_k, cache_v, bias, dl, gn_row)


def _mix_ffn_kernel(x_ref, og_ref, od_ref, wo_ref, g_ref, wi_ref, wf_ref, gf_ref, y_ref, *, final_norm):
    x = (x_ref[...]
         + jnp.dot(og_ref[...], wo_ref[:GLA_WIDTH, :], preferred_element_type=F32)
         + jnp.dot(od_ref[...], wo_ref[GLA_WIDTH:, :], preferred_element_type=F32))
    hn = _rms(x, g_ref[...]).astype(BF16)
    y_ref[...] = x
    for c0 in range(0, D_FF, FF_CHUNK):
        c1 = min(c0 + FF_CHUNK, D_FF)
        gate = jnp.dot(hn, wi_ref[:, c0:c1], preferred_element_type=F32)
        up = jnp.dot(hn, wi_ref[:, D_FF + c0:D_FF + c1], preferred_element_type=F32)
        hmid = (gate * (1.0 / (1.0 + jnp.exp(-gate))) * up).astype(BF16)
        y_ref[...] += jnp.dot(hmid, wf_ref[c0:c1, :], preferred_element_type=F32)
    if final_norm:
        y_ref[...] = _rms(y_ref[...], gf_ref[...])


def _layer_block(a, layer):
    return pl.BlockSpec((None,) + a.shape[1:], lambda i: (layer,) + (0,) * (a.ndim - 1),
                        pipeline_mode=pl.Buffered(1))


def _mix_ffn(x, og, od, w_out, g, w_ffn_in, w_ffn_out, g_final, *, layer, final_norm, tm):
    m = x.shape[0]
    row = lambda w: pl.BlockSpec((tm, w), lambda i: (i, 0))
    return pl.pallas_call(
        functools.partial(_mix_ffn_kernel, final_norm=final_norm),
        grid=(m // tm,),
        in_specs=[row(D_MODEL), row(GLA_WIDTH), row(DIFF_WIDTH), _layer_block(w_out, layer), _layer_block(g, layer),
                  _layer_block(w_ffn_in, layer), _layer_block(w_ffn_out, layer),
                  pl.BlockSpec(g_final.shape, lambda i: (0, 0))],
        out_specs=row(D_MODEL),
        out_shape=jax.ShapeDtypeStruct((m, D_MODEL), F32),
        compiler_params=pltpu.CompilerParams(dimension_semantics=("arbitrary",), vmem_limit_bytes=VMEM_LIMIT),
        name="mix_ffn",
    )(x, og, od, w_out, g, w_ffn_in, w_ffn_out, g_final)


def _t5_bucket(rel):
    nb = T5_BUCKETS // 2
    max_exact = nb // 2
    ret = jnp.where(rel > 0, nb, 0)
    n = jnp.abs(rel)
    nf = jnp.maximum(n, 1).astype(F32)
    large = max_exact + (jnp.log(nf / max_exact) / math.log(T5_MAX_DIST / max_exact)
                         * (nb - max_exact)).astype(jnp.int32)
    large = jnp.minimum(large, nb - 1)
    return ret + jnp.where(n < max_exact, n, large)


def _bias_minus_far(rel_bias, rel):
    table = rel_bias.astype(F32)
    table = (table - table[_t5_bucket(jnp.full((), -T5_MAX_DIST, jnp.int32))]) * LOG2E
    hit = _t5_bucket(rel)[..., None, None] == jnp.arange(T5_BUCKETS, dtype=jnp.int32)[:, None]
    return jnp.moveaxis(jnp.sum(jnp.where(hit, table, 0.0), axis=-2), -1, 0)


def _prompt_bias_tiles(rel_bias):
    s = jnp.arange(ATT_TK, dtype=jnp.int32)[:, None]
    t = jnp.arange(ATT_TQ, dtype=jnp.int32)[None, :]
    period = 1 << (ATT_TK + ATT_TQ - 1).bit_length()
    j = jnp.arange(period, dtype=jnp.int32)
    t_minus_s = jnp.where(j < ATT_TQ, j, j - period)
    tiles = []
    for d in range(BIAS_TILES):
        by_offset = _bias_minus_far(rel_bias, -t_minus_s - d * ATT_TQ)
        bias = jnp.tile(by_offset, (1, ATT_TK))[:, :ATT_TK * (period - 1)]
        bias = bias.reshape(-1, ATT_TK, period - 1)[:, :, :ATT_TQ]
        allowed = (s // CHUNK - d * (ATT_TQ // CHUNK)) <= (t // CHUNK)
        tiles.append(jnp.where(allowed, bias, NEG_INF))
    tiles.append(jnp.full_like(tiles[0], NEG_INF))
    return jnp.stack(tiles, axis=1)


def _sample_bias_tile(rel_bias, t_new, n_near_past):
    q = jnp.arange(t_new, dtype=jnp.int32)[:, None]
    k = jnp.arange(n_near_past + t_new, dtype=jnp.int32)[None, :] - n_near_past
    return _bias_minus_far(rel_bias, k - q)


def _prepare_params(w_in, gla_w_alpha2, gla_b_alpha, norm_mix_g, norm_ffn_g, w_out, w_ffn_in, w_ffn_out,
                    final_norm_g):
    depth = w_in.shape[0]
    c_glr = _COLS["glr"][0]
    pad = jnp.zeros((depth, w_in.shape[1], GATE_PAD - GLA_GATE_RANK), w_in.dtype)
    w_in_p = jnp.concatenate([w_in[:, :, :c_glr + GLA_GATE_RANK], pad, w_in[:, :, c_glr + GLA_GATE_RANK:]], axis=2)
    wa = jnp.concatenate([gla_w_alpha2, jnp.zeros((depth, GATE_PAD - GLA_GATE_RANK, GLA_QK), gla_w_alpha2.dtype)],
                         axis=1)
    return dict(w_in=w_in_p.astype(BF16), wa=wa.astype(BF16), ba=gla_b_alpha.reshape(depth, 1, GLA_QK),
                norm_mix_g=norm_mix_g.reshape(depth, 1, D_MODEL), norm_ffn_g=norm_ffn_g.reshape(depth, 1, D_MODEL),
                w_out=w_out.astype(BF16), w_ffn_in=w_ffn_in.astype(BF16), w_ffn_out=w_ffn_out.astype(BF16),
                final_norm_g=final_norm_g.reshape(1, D_MODEL))


def _trunk(x, cache_k, cache_v, gla_state, params, *, prompt):
    batch, seq, _ = x.shape
    m = batch * seq
    x = x.reshape(m, D_MODEL)
    depth = params["w_in"].shape[0]
    tm_proj = min(512, m)
    tm_ffn = min(512, m)
    new_k, new_v, new_s = [], [], []
    if prompt:
        bias = _prompt_bias_tiles(params["rel_bias"])
    else:
        past = cache_k.shape[2]
        n_near_past = LANES
        bias = _sample_bias_tile(params["rel_bias"], seq, n_near_past)
        cache_k = cache_k.reshape(-1, DIFF_DV)
        cache_v = cache_v.reshape(-1, DIFF_DV)
    for l in range(depth):
        lam_init = 0.8 - 0.6 * math.exp(-0.3 * l)
        outs = _inproj(x, params["norm_mix_g"], params["w_in"], params["wa"], params["ba"],
                       layer=l, tm=tm_proj, with_attn_layouts=prompt)
        gq, gk, gv, gg, la, dq, dk, dv = outs[:8]
        s0 = jnp.zeros((batch, GLA_HEADS, GLA_DK, GLA_DV), F32) if gla_state is None else gla_state[l]
        o_gla, s_fin = _gla(gq, gk, gv, la, gg, s0, params["gla_norm_g"][l].reshape(1, GLA_DV),
                            batch=batch, seq=seq)
        dl = params["diff_lambda"][l]
        if prompt:
            dkb, dvt = outs[8:]
            o_diff = _attn_prompt(dq, dkb, dvt, bias, dl, params["diff_norm_g"][l].reshape(DIFF_DV, 1),
                                  lam_init=lam_init)
        else:
            o_diff = _attn_sample(dq, dk, dv, cache_k, cache_v, bias, dl,
                                  params["diff_norm_g"][l].reshape(1, DIFF_DV), layer=l, batch=batch, past=past,
                                  lam_init=lam_init, n_near_past=n_near_past)
        x = _mix_ffn(x, o_gla, o_diff, params["w_out"], params["norm_ffn_g"], params["w_ffn_in"],
                     params["w_ffn_out"], params["final_norm_g"], layer=l, final_norm=(l == depth - 1), tm=tm_ffn)
        new_k.append(dk.reshape(batch, seq, DIFF_HEADS, DIFF_DV))
        new_v.append(dv.reshape(batch, seq, DIFF_HEADS, DIFF_DV))
        new_s.append(s_fin)
    return x.reshape(batch, seq, D_MODEL), jnp.stack(new_k), jnp.stack(new_v), jnp.stack(new_s)


def kernel(x_prompt, x_sample, cache_diff_k, cache_diff_v, state_gla, w_in, gla_w_alpha2, gla_b_alpha, gla_norm_g,
           diff_lambda, diff_norm_g, w_out, norm_mix_g, norm_ffn_g, w_ffn_in, w_ffn_out, final_norm_g, rel_bias):
    params = _prepare_params(w_in, gla_w_alpha2, gla_b_alpha, norm_mix_g, norm_ffn_g, w_out, w_ffn_in, w_ffn_out,
                             final_norm_g)
    params.update(gla_norm_g=gla_norm_g, diff_lambda=diff_lambda, diff_norm_g=diff_norm_g, rel_bias=rel_bias)
    y_p, k_p, v_p, s_p = _trunk(x_prompt, None, None, None, params, prompt=True)
    y_s, k_s, v_s, s_s = _trunk(x_sample, cache_diff_k, cache_diff_v, state_gla, params, prompt=False)
    return (y_p, y_s, k_p, v_p, s_p, k_s, v_s, s_s)
```

```python
import functools
import math

import jax
import jax.numpy as jnp
from jax import lax
from jax.experimental import pallas as pl
from jax.experimental.pallas import tpu as pltpu

F32 = jnp.float32
BF16 = jnp.bfloat16

D_MODEL = 1024
CHUNK = 64
GLA_HEADS = 4
GLA_DK = 64
GLA_DV = 128
GLA_WIDTH = GLA_HEADS * GLA_DV
GLA_QK = GLA_HEADS * GLA_DK
GLA_GATE_RANK = 16
GLA_GATE_NORMALIZER = 16.0
DIFF_HEADS = 4
DIFF_DV = 128
DIFF_HEAD_DIM = 64
DIFF_WIDTH = DIFF_HEADS * DIFF_DV
T5_BUCKETS = 32
T5_MAX_DIST = 128
D_FF = 2816
NEG_INF = -1e30
RMS_EPS = 1e-6

LANES = 128
GATE_PAD = LANES
VMEM_LIMIT = 56 * 1024 * 1024

_COLS = {}
_off = 0
for _name, _w in (("gq", GLA_QK), ("gk", GLA_QK), ("gv", GLA_WIDTH), ("gg", GLA_WIDTH), ("glr", GATE_PAD),
                  ("dq", DIFF_WIDTH), ("dk", DIFF_WIDTH), ("dv", DIFF_WIDTH)):
    _COLS[_name] = (_off, _w)
    _off += _w
D_IN_PAD = _off

ATT_TQ = 256
ATT_TK = 512
ATT_STEPS_PER_QBLOCK = ATT_TK // ATT_TQ
ATT_CHUNK = 256
ATT_VT_ROWS = DIFF_DV + 16
ATT_HEADS_PER_STEP = 2
ATT_QBLOCKS_PER_STEP = 2
BIAS_TILES = -(-(ATT_TK + T5_MAX_DIST) // ATT_TQ) + 1
LOG2E = math.log2(math.e)
GLA_CHUNKS_PER_STEP = 8
GLA_CHUNKS_PER_TRIP = 2
FF_CHUNK = 512


def _nt_dot(a, b):
    return lax.dot_general(a, b, (((1,), (1,)), ((), ())), preferred_element_type=F32)


def _rms(x, g):
    return x * lax.rsqrt(jnp.mean(x * x, axis=-1, keepdims=True) + RMS_EPS) * g


def _inproj_kernel(x_ref, g_ref, w_ref, wa_ref, ba_ref,
                   gq_ref, gk_ref, gv_ref, gg_ref, la_ref, dq_ref, dk_ref, dv_ref, *attn_refs):
    xn = _rms(x_ref[...], g_ref[...]).astype(BF16)

    def proj(name):
        c0, w = _COLS[name]
        return jnp.dot(xn, w_ref[:, c0:c0 + w], preferred_element_type=F32)

    gq_ref[...] = proj("gq") * (GLA_DK ** -0.5)
    gk_ref[...] = proj("gk")
    gv_ref[...] = proj("gv")
    gg_ref[...] = proj("gg")
    z = jnp.dot(proj("glr").astype(BF16), wa_ref[...], preferred_element_type=F32) + ba_ref[...]
    la_ref[...] = (jnp.minimum(z, 0.0) - jnp.log1p(jnp.exp(-jnp.abs(z)))) / GLA_GATE_NORMALIZER
    dq_ref[...] = (proj("dq") * (DIFF_HEAD_DIM ** -0.5 * LOG2E)).astype(BF16)
    dk = proj("dk")
    dv = proj("dv")
    tm = dk.shape[0]
    for h in range(DIFF_HEADS):
        dk_ref[pl.ds(h, tm, stride=DIFF_HEADS), :] = dk[:, h * DIFF_DV:(h + 1) * DIFF_DV]
        dv_ref[pl.ds(h, tm, stride=DIFF_HEADS), :] = dv[:, h * DIFF_DV:(h + 1) * DIFF_DV]
    if attn_refs:
        dkb_ref, dvt_ref = attn_refs
        dkb_ref[...] = dk.astype(BF16)
        for r in range(tm // ATT_TK):
            vt = dv[r * ATT_TK:(r + 1) * ATT_TK, :].T
            for h in range(DIFF_HEADS):
                pad_row = lax.broadcasted_iota(jnp.int32, (ATT_VT_ROWS - DIFF_DV, ATT_TK), 0)
                ones_row = jnp.where(pad_row == 0, 1.0, 0.0)
                dvt_ref[h, r] = jnp.concatenate([vt[h * DIFF_DV:(h + 1) * DIFF_DV, :], ones_row],
                                                axis=0).astype(BF16)


def _inproj(x, g, w_in, wa, ba, *, layer, tm, with_attn_layouts):
    m = x.shape[0]
    row = lambda w: pl.BlockSpec((tm, w), lambda i: (i, 0))
    full = lambda a: _layer_block(a, layer)
    out_shape = [jax.ShapeDtypeStruct((m, GLA_QK), F32), jax.ShapeDtypeStruct((m, GLA_QK), F32),
                 jax.ShapeDtypeStruct((m, GLA_WIDTH), F32), jax.ShapeDtypeStruct((m, GLA_WIDTH), F32),
                 jax.ShapeDtypeStruct((m, GLA_QK), F32), jax.ShapeDtypeStruct((m, DIFF_WIDTH), BF16),
                 jax.ShapeDtypeStruct((m * DIFF_HEADS, DIFF_DV), F32),
                 jax.ShapeDtypeStruct((m * DIFF_HEADS, DIFF_DV), F32)]
    heads_rows = pl.BlockSpec((tm * DIFF_HEADS, DIFF_DV), lambda i: (i, 0))
    out_specs = [row(GLA_QK), row(GLA_QK), row(GLA_WIDTH), row(GLA_WIDTH), row(GLA_QK), row(DIFF_WIDTH),
                 heads_rows, heads_rows]
    if with_attn_layouts:
        out_shape += [jax.ShapeDtypeStruct((m, DIFF_WIDTH), BF16),
                      jax.ShapeDtypeStruct((DIFF_HEADS, m // ATT_TK, ATT_VT_ROWS, ATT_TK), BF16)]
        out_specs += [row(DIFF_WIDTH),
                      pl.BlockSpec((DIFF_HEADS, tm // ATT_TK, ATT_VT_ROWS, ATT_TK), lambda i: (0, i, 0, 0))]
    return pl.pallas_call(
        _inproj_kernel,
        grid=(m // tm,),
        in_specs=[row(D_MODEL), full(g), full(w_in), full(wa), full(ba)],
        out_specs=out_specs,
        out_shape=out_shape,
        compiler_params=pltpu.CompilerParams(dimension_semantics=("arbitrary",), vmem_limit_bytes=VMEM_LIMIT),
        name="inproj",
    )(x, g, w_in, wa, ba)


def _level_ref_rows(b, h):
    if h >= 8:
        pieces = []
        for blk in range(CHUNK // (2 * h)):
            ref = blk * 2 * h + h
            pieces.append(jnp.broadcast_to(b[ref:ref + 1, :], (2 * h, b.shape[1])))
        return jnp.concatenate(pieces, axis=0)
    b3 = b.reshape(CHUNK // 8, 8, b.shape[1])
    sub = lax.broadcasted_iota(jnp.int32, b3.shape, 1)
    pick = lambda s: jnp.broadcast_to(b3[:, s:s + 1, :], b3.shape)
    if h == 4:
        out = pick(4)
    elif h == 2:
        out = jnp.where(sub < 4, pick(2), pick(6))
    else:
        out = jnp.where(sub < 2, pick(1), jnp.where(sub < 4, pick(3), jnp.where(sub < 6, pick(5), pick(7))))
    return out.reshape(b.shape)


def _gla_chunk(q, k, v, la, gg, gn, s_prev):
    row = lax.broadcasted_iota(jnp.int32, (CHUNK, CHUNK), 0)
    col = lax.broadcasted_iota(jnp.int32, (CHUNK, CHUNK), 1)
    tril = (row >= col).astype(F32)
    b = jnp.dot(tril, la, preferred_element_type=F32, precision=lax.Precision.HIGHEST)
    b_last = b[CHUNK - 1:CHUNK, :]

    r256 = lax.broadcasted_iota(jnp.int32, (CHUNK, GLA_QK), 0)
    bd_r = lax.broadcasted_iota(jnp.int32, (GLA_QK, GLA_QK), 0) // GLA_DK
    bd_c = lax.broadcasted_iota(jnp.int32, (GLA_QK, GLA_QK), 1) // GLA_DK
    bd_mask = bd_r == bd_c
    t_idx = lax.broadcasted_iota(jnp.int32, (CHUNK, GLA_QK), 0)
    s_idx = lax.broadcasted_iota(jnp.int32, (CHUNK, GLA_QK), 1) % CHUNK

    def head_scores(a_bf, b_bf):
        bbd = jnp.where(bd_mask, jnp.concatenate([b_bf] * GLA_HEADS, axis=0), jnp.zeros((), BF16))
        return _nt_dot(a_bf, bbd)

    scores = jnp.where(t_idx == s_idx, head_scores(q.astype(BF16), k.astype(BF16)), 0.0)
    h = CHUNK // 2
    while h >= 1:
        upper = (r256 % (2 * h)) >= h
        bref = _level_ref_rows(b, h)
        e = jnp.exp(jnp.where(upper, b - bref, bref - b))
        a_l = jnp.where(upper, q * e, 0.0).astype(BF16)
        b_l = jnp.where(upper, 0.0, k * e).astype(BF16)
        same_block = (t_idx // (2 * h)) == (s_idx // (2 * h))
        scores = scores + jnp.where(same_block, head_scores(a_l, b_l), 0.0)
        h //= 2

    q_int = (q * jnp.exp(b)).astype(BF16)
    lhs = jnp.concatenate([scores.astype(BF16), q_int], axis=1)
    lane_head = (lax.broadcasted_iota(jnp.int32, lhs.shape, 1) % GLA_QK) // GLA_DK
    lhs_stack = jnp.concatenate(
        [jnp.where(lane_head == hd, lhs, jnp.zeros((), BF16)) for hd in range(GLA_HEADS)], axis=0)
    v_bf = v.astype(BF16)
    v_stack = jnp.concatenate([v_bf[:, hd * GLA_DV:(hd + 1) * GLA_DV] for hd in range(GLA_HEADS)], axis=0)
    rhs = jnp.concatenate([v_stack, s_prev.astype(BF16)], axis=0)
    o_stack = jnp.dot(lhs_stack, rhs, preferred_element_type=F32)

    k_dec = k * jnp.exp(b_last - b)
    kd_pad = jnp.concatenate([k_dec, jnp.broadcast_to(jnp.exp(b_last), (CHUNK, GLA_QK))], axis=0)
    kd_t = kd_pad.T
    dec_col = kd_t[:, CHUNK:CHUNK + 1]
    s_lane = lax.broadcasted_iota(jnp.int32, kd_t.shape, 1)
    kd_t_bf = jnp.where(s_lane < CHUNK, kd_t, 0.0).astype(BF16)
    zeros_v = jnp.zeros((LANES - CHUNK, GLA_DV), BF16)
    upd = []
    for hd in range(GLA_HEADS):
        v_h = jnp.concatenate([v_bf[:, hd * GLA_DV:(hd + 1) * GLA_DV], zeros_v], axis=0)
        upd.append(jnp.dot(kd_t_bf[hd * GLA_DK:(hd + 1) * GLA_DK, :], v_h, preferred_element_type=F32))
    s_new = dec_col * s_prev + jnp.concatenate(upd, axis=0)

    outs = []
    for hd in range(GLA_HEADS):
        o_h = o_stack[hd * CHUNK:(hd + 1) * CHUNK, :]
        outs.append(_rms(o_h, gn))
    o = jnp.concatenate(outs, axis=1)
    o = o * (gg * (1.0 / (1.0 + jnp.exp(-gg))))
    return o.astype(BF16), s_new


def _gla_kernel(q_ref, k_ref, v_ref, la_ref, gg_ref, s0_ref, gn_ref, o_ref, sfin_ref, s_scr, *, chunks):
    j = pl.program_id(1)

    @pl.when(j == 0)
    def _():
        for hd in range(GLA_HEADS):
            s_scr[hd * GLA_DK:(hd + 1) * GLA_DK, :] = s0_ref[0, hd]

    gn = gn_ref[...]

    unroll = GLA_CHUNKS_PER_TRIP if chunks % GLA_CHUNKS_PER_TRIP == 0 else 1

    def body(c, carry):
        s = s_scr[...]
        for u in range(unroll):
            r0 = pl.multiple_of((c * unroll + u) * CHUNK, CHUNK)
            rows = pl.ds(r0, CHUNK)
            o, s = _gla_chunk(q_ref[rows, :], k_ref[rows, :], v_ref[rows, :], la_ref[rows, :], gg_ref[rows, :],
                              gn, s)
            o_ref[rows, :] = o
        s_scr[...] = s
        return carry

    lax.fori_loop(0, chunks // unroll, body, 0)

    @pl.when(j == pl.num_programs(1) - 1)
    def _():
        for hd in range(GLA_HEADS):
            sfin_ref[0, hd] = s_scr[hd * GLA_DK:(hd + 1) * GLA_DK, :]


def _gla(gq, gk, gv, la, gg, s0, gn, *, batch, seq):
    n_chunks = seq // CHUNK
    chunks = min(GLA_CHUNKS_PER_STEP, n_chunks)
    steps = n_chunks // chunks
    tm = chunks * CHUNK
    row = lambda w: pl.BlockSpec((tm, w), lambda bi, j: (bi * steps + j, 0))
    st = pl.BlockSpec((1, GLA_HEADS, GLA_DK, GLA_DV), lambda bi, j: (bi, 0, 0, 0))
    return pl.pallas_call(
        functools.partial(_gla_kernel, chunks=chunks),
        grid=(batch, steps),
        in_specs=[row(GLA_QK), row(GLA_QK), row(GLA_WIDTH), row(GLA_QK), row(GLA_WIDTH), st,
                  pl.BlockSpec((1, GLA_DV), lambda bi, j: (0, 0))],
        out_specs=[row(GLA_WIDTH), st],
        out_shape=[jax.ShapeDtypeStruct((batch * seq, GLA_WIDTH), BF16),
                   jax.ShapeDtypeStruct((batch, GLA_HEADS, GLA_DK, GLA_DV), F32)],
        scratch_shapes=[pltpu.VMEM((GLA_QK, GLA_DV), F32)],
        compiler_params=pltpu.CompilerParams(dimension_semantics=("arbitrary", "arbitrary"),
                                             vmem_limit_bytes=VMEM_LIMIT),
        name="gla",
    )(gq, gk, gv, la, gg, s0, gn)


def _lambda(dl, lam_init):
    s01 = jnp.sum(dl[0:1, :] * dl[1:2, :], axis=1, keepdims=True)
    s23 = jnp.sum(dl[2:3, :] * dl[3:4, :], axis=1, keepdims=True)
    return jnp.exp(s01) - jnp.exp(s23) + lam_init


def _split_maps(q):
    lane = lax.broadcasted_iota(jnp.int32, q.shape, 1)
    zero = jnp.zeros((), q.dtype)
    return jnp.concatenate([jnp.where(lane < DIFF_HEAD_DIM, q, zero), jnp.where(lane >= DIFF_HEAD_DIM, q, zero)],
                           axis=0)


def _attn_prompt_kernel(q_ref, k_ref, vt_ref, bias_ref, dl_ref, gn_ref, o_ref, wq_scr, s_a, s_b, p_a, p_b,
                        acc_scr, *, lam_init):
    i = pl.program_id(1)
    chains = [(hd, qb) for hd in range(acc_scr.shape[0] // ATT_QBLOCKS_PER_STEP)
              for qb in range(ATT_QBLOCKS_PER_STEP)]
    n_steps = (i * ATT_QBLOCKS_PER_STEP) // ATT_STEPS_PER_QBLOCK + 1
    n_pairs = (n_steps + 1) // 2
    q_rows = lambda qb: slice(qb * ATT_TQ, (qb + 1) * ATT_TQ)
    h_cols = lambda hd: slice(hd * DIFF_DV, (hd + 1) * DIFF_DV)
    for g, (hd, qb) in enumerate(chains):
        qt = q_ref[q_rows(qb), h_cols(hd)].astype(F32).T
        row = lax.broadcasted_iota(jnp.int32, qt.shape, 0)
        wq_scr[g] = jnp.concatenate([jnp.where(row < DIFF_HEAD_DIM, qt, 0.0),
                                     jnp.where(row >= DIFF_HEAD_DIM, qt, 0.0)], axis=1).astype(BF16)
    acc_scr[...] = jnp.zeros_like(acc_scr)
    p_b[...] = jnp.zeros_like(p_b)

    def qk_chunk(g, t, c, s_scr):
        hd, qb = chains[g]
        tc = jnp.minimum(t, n_steps - 1)
        k0 = pl.multiple_of(tc * ATT_TK + c * ATT_CHUNK, ATT_CHUNK)
        behind = i * ATT_QBLOCKS_PER_STEP + qb - ATT_STEPS_PER_QBLOCK * tc
        tile = jnp.where(t < n_steps, jnp.minimum(behind, BIAS_TILES - 1), BIAS_TILES)
        rows = slice(c * ATT_CHUNK, (c + 1) * ATT_CHUNK)
        bias = bias_ref[hd, tile, rows, :]
        s = (jnp.dot(k_ref[pl.ds(k0, ATT_CHUNK), h_cols(hd)], wq_scr[g],
                     preferred_element_type=F32) + jnp.concatenate([bias, bias], axis=1))
        s_scr[g, rows, :] = s
        return jnp.max(s, axis=0, keepdims=True)

    def softmax_chunk(g, c, s_scr, p_scr, m_new):
        rows = slice(c * ATT_CHUNK, (c + 1) * ATT_CHUNK)
        p_scr[g, rows, :] = jnp.exp2(s_scr[g, rows, :] - m_new).astype(BF16)

    def pv_chunk(g, t, c, p_scr, alpha):
        tc = jnp.clip(t, 0, n_steps - 1)
        cols = slice(c * ATT_CHUNK, (c + 1) * ATT_CHUNK)
        d = jnp.dot(vt_ref[chains[g][0], tc, :, cols], p_scr[g, cols, :], preferred_element_type=F32)
        if c == 0:
            acc_scr[g] = acc_scr[g] * alpha + d
        else:
            acc_scr[g] += d

    def half_step(g, t, s_cur, p_cur, s_nxt, p_prev, m, mb_cur, alpha_prev):
        m_new = jnp.maximum(m, mb_cur)
        alpha = jnp.exp2(m - m_new)
        mb_nxt = None
        for c in range(ATT_TK // ATT_CHUNK):
            softmax_chunk(g, c, s_cur, p_cur, m_new)
            pv_chunk(g, t - 1, c, p_prev, alpha_prev)
            mbc = qk_chunk(g, t + 1, c, s_nxt)
            mb_nxt = mbc if mb_nxt is None else jnp.maximum(mb_nxt, mbc)
        return m_new, mb_nxt, alpha

    def body(jj, carry):
        t = 2 * jj
        out = []
        for g in range(len(chains)):
            m, mb_a, alpha_b = carry[g]
            m, mb_b, alpha_a = half_step(g, t, s_a, p_a, s_b, p_b, m, mb_a, alpha_b)
            m, mb_a, alpha_b = half_step(g, t + 1, s_b, p_b, s_a, p_a, m, mb_b, alpha_a)
            out.append((m, mb_a, alpha_b))
        return tuple(out)

    init = []
    for g in range(len(chains)):
        mb0 = None
        for c in range(ATT_TK // ATT_CHUNK):
            mbc = qk_chunk(g, 0, c, s_a)
            mb0 = mbc if mb0 is None else jnp.maximum(mb0, mbc)
        init.append((jnp.full((1, 2 * ATT_TQ), NEG_INF, F32), mb0, jnp.ones((1, 2 * ATT_TQ), F32)))
    final = lax.fori_loop(0, n_pairs, body, tuple(init))

    lam = _lambda(dl_ref[...], lam_init)
    for g, (hd, qb) in enumerate(chains):
        alpha_b = final[g][2]
        for c in range(ATT_TK // ATT_CHUNK):
            pv_chunk(g, 2 * n_pairs - 1, c, p_b, alpha_b)
        acc = acc_scr[g]
        o = acc[:DIFF_DV, :] / acc[DIFF_DV:DIFF_DV + 1, :]
        od = o[:, :ATT_TQ] - lam * o[:, ATT_TQ:]
        y = od * lax.rsqrt(jnp.mean(od * od, axis=0, keepdims=True) + RMS_EPS) * gn_ref[...] * (1.0 - lam_init)
        o_ref[q_rows(qb), h_cols(hd)] = y.T.astype(BF16)


def _attn_prompt(dq, dkb, dvt, bias, dl, gn_col, *, lam_init):
    t = dq.shape[0]
    g = ATT_HEADS_PER_STEP
    qblocks = ATT_QBLOCKS_PER_STEP
    tq = qblocks * ATT_TQ
    assert t % ATT_TK == 0 and ATT_TK % tq == 0 and t % tq == 0 and DIFF_HEADS % g == 0
    n_chains = g * qblocks
    tile = lambda dtype: pltpu.VMEM((n_chains, ATT_TK, 2 * ATT_TQ), dtype)
    once = dict(pipeline_mode=pl.Buffered(1))
    return pl.pallas_call(
        functools.partial(_attn_prompt_kernel, lam_init=lam_init),
        grid=(DIFF_HEADS // g, t // tq),
        in_specs=[pl.BlockSpec((tq, g * DIFF_DV), lambda h, i: (i, h)),
                  pl.BlockSpec((t, g * DIFF_DV), lambda h, i: (0, h), **once),
                  pl.BlockSpec((g, t // ATT_TK, ATT_VT_ROWS, ATT_TK), lambda h, i: (h, 0, 0, 0), **once),
                  pl.BlockSpec((g,) + bias.shape[1:], lambda h, i: (h, 0, 0, 0), **once),
                  pl.BlockSpec(dl.shape, lambda h, i: (0, 0)),
                  pl.BlockSpec(gn_col.shape, lambda h, i: (0, 0))],
        out_specs=pl.BlockSpec((tq, g * DIFF_DV), lambda h, i: (i, h)),
        out_shape=jax.ShapeDtypeStruct((t, DIFF_WIDTH), BF16),
        scratch_shapes=[pltpu.VMEM((n_chains, DIFF_DV, 2 * ATT_TQ), BF16), tile(F32), tile(F32), tile(BF16),
                        tile(BF16), pltpu.VMEM((n_chains, ATT_VT_ROWS, 2 * ATT_TQ), F32)],
        compiler_params=pltpu.CompilerParams(dimension_semantics=("arbitrary", "arbitrary"),
                                             vmem_limit_bytes=VMEM_LIMIT),
        name="attn_prompt",
    )(dq, dkb, dvt, bias, dl, gn_col)


def _attn_sample_kernel(q_ref, kn_ref, vn_ref, kc_ref, vc_ref, bias_ref, dl_ref, gn_ref, o_ref, acc_scr, *,
                        lam_init, n_near_past):
    t = q_ref.shape[0]
    past = kc_ref.shape[0] // DIFF_HEADS
    n_far = past - n_near_past
    lam = _lambda(dl_ref[...], lam_init)
    for h in range(DIFF_HEADS):
        head_rows = lambda ref, first, n: ref[pl.ds(first * DIFF_HEADS + h, n, stride=DIFF_HEADS), :]
        wq = _split_maps(q_ref[:, h * DIFF_DV:(h + 1) * DIFF_DV])
        k_far = head_rows(kc_ref, 0, n_far).astype(BF16)
        v_far = head_rows(vc_ref, 0, n_far).astype(BF16)
        k_near = jnp.concatenate([head_rows(kc_ref, n_far, n_near_past), head_rows(kn_ref, 0, t)],
                                 axis=0).astype(BF16)
        v_near = jnp.concatenate([head_rows(vc_ref, n_far, n_near_past), head_rows(vn_ref, 0, t)],
                                 axis=0).astype(BF16)
        bias = bias_ref[h]
        s_far = _nt_dot(wq, k_far)
        s_near = _nt_dot(wq, k_near) + jnp.concatenate([bias, bias], axis=0)
        m = jnp.maximum(jnp.max(s_far, axis=1, keepdims=True), jnp.max(s_near, axis=1, keepdims=True))
        p_far = jnp.exp2(s_far - m)
        p_near = jnp.exp2(s_near - m)
        l = jnp.sum(p_far, axis=1, keepdims=True) + jnp.sum(p_near, axis=1, keepdims=True)
        acc_scr[...] = jnp.dot(p_far.astype(BF16), v_far, preferred_element_type=F32)
        acc_scr[...] += jnp.dot(p_near.astype(BF16), v_near, preferred_element_type=F32)
        o = acc_scr[...] / l
        od = o[:t, :] - lam * o[t:, :]
        o_ref[:, h * DIFF_DV:(h + 1) * DIFF_DV] = (_rms(od, gn_ref[...]) * (1.0 - lam_init)).astype(BF16)


def _attn_sample(dq, dk, dv, cache_k, cache_v, bias, dl, gn_row, *, layer, batch, past, lam_init, n_near_past):
    t = dq.shape[0] // batch
    new = lambda: pl.BlockSpec((t * DIFF_HEADS, DIFF_DV), lambda b: (b, 0))
    cache = lambda: pl.BlockSpec((past * DIFF_HEADS, DIFF_DV), lambda b: (layer * batch + b, 0))
    wide = lambda: pl.BlockSpec((t, DIFF_WIDTH), lambda b: (b, 0))
    const = lambda a: pl.BlockSpec(a.shape, lambda b: (0,) * a.ndim)
    return pl.pallas_call(
        functools.partial(_attn_sample_kernel, lam_init=lam_init, n_near_past=n_near_past),
        grid=(batch,),
        in_specs=[wide(), new(), new(), cache(), cache(), const(bias), const(dl), const(gn_row)],
        out_specs=wide(),
        out_shape=jax.ShapeDtypeStruct((batch * t, DIFF_WIDTH), BF16),
        scratch_shapes=[pltpu.VMEM((2 * t, DIFF_DV), F32)],
        compiler_params=pltpu.CompilerParams(dimension_semantics=("arbitrary",), vmem_limit_bytes=VMEM_LIMIT),
        name="attn_sample",
    )(dq, dk, dv, cache_k, cache_v, bias, dl, gn_row)


def _mix_ffn_kernel(x_ref, og_ref, od_ref, wo_ref, g_ref, wi_ref, wf_ref, gf_ref, y_ref, *, final_norm):
    x = (x_ref[...]
         + jnp.dot(og_ref[...], wo_ref[:GLA_WIDTH, :], preferred_element_type=F32)
         + jnp.dot(od_ref[...], wo_ref[GLA_WIDTH:, :], preferred_element_type=F32))
    hn = _rms(x, g_ref[...]).astype(BF16)
    y_ref[...] = x
    for c0 in range(0, D_FF, FF_CHUNK):
        c1 = min(c0 + FF_CHUNK, D_FF)
        gate = jnp.dot(hn, wi_ref[:, c0:c1], preferred_element_type=F32)
        up = jnp.dot(hn, wi_ref[:, D_FF + c0:D_FF + c1], preferred_element_type=F32)
        hmid = (gate * (1.0 / (1.0 + jnp.exp(-gate))) * up).astype(BF16)
        y_ref[...] += jnp.dot(hmid, wf_ref[c0:c1, :], preferred_element_type=F32)
    if final_norm:
        y_ref[...] = _rms(y_ref[...], gf_ref[...])


def _layer_block(a, layer):
    return pl.BlockSpec((None,) + a.shape[1:], lambda i: (layer,) + (0,) * (a.ndim - 1),
                        pipeline_mode=pl.Buffered(1))


def _mix_ffn(x, og, od, w_out, g, w_ffn_in, w_ffn_out, g_final, *, layer, final_norm, tm):
    m = x.shape[0]
    row = lambda w: pl.BlockSpec((tm, w), lambda i: (i, 0))
    return pl.pallas_call(
        functools.partial(_mix_ffn_kernel, final_norm=final_norm),
        grid=(m // tm,),
        in_specs=[row(D_MODEL), row(GLA_WIDTH), row(DIFF_WIDTH), _layer_block(w_out, layer), _layer_block(g, layer),
                  _layer_block(w_ffn_in, layer), _layer_block(w_ffn_out, layer),
                  pl.BlockSpec(g_final.shape, lambda i: (0, 0))],
        out_specs=row(D_MODEL),
        out_shape=jax.ShapeDtypeStruct((m, D_MODEL), F32),
        compiler_params=pltpu.CompilerParams(dimension_semantics=("arbitrary",), vmem_limit_bytes=VMEM_LIMIT),
        name="mix_ffn",
    )(x, og, od, w_out, g, w_ffn_in, w_ffn_out, g_final)


def _t5_bucket(rel):
    nb = T5_BUCKETS // 2
    max_exact = nb // 2
    ret = jnp.where(rel > 0, nb, 0)
    n = jnp.abs(rel)
    nf = jnp.maximum(n, 1).astype(F32)
    large = max_exact + (jnp.log(nf / max_exact) / math.log(T5_MAX_DIST / max_exact)
                         * (nb - max_exact)).astype(jnp.int32)
    large = jnp.minimum(large, nb - 1)
    return ret + jnp.where(n < max_exact, n, large)


def _bias_minus_far(rel_bias, rel):
    table = rel_bias.astype(F32)
    table = (table - table[_t5_bucket(jnp.full((), -T5_MAX_DIST, jnp.int32))]) * LOG2E
    hit = _t5_bucket(rel)[..., None, None] == jnp.arange(T5_BUCKETS, dtype=jnp.int32)[:, None]
    return jnp.moveaxis(jnp.sum(jnp.where(hit, table, 0.0), axis=-2), -1, 0)


def _prompt_bias_tiles(rel_bias):
    s = jnp.arange(ATT_TK, dtype=jnp.int32)[:, None]
    t = jnp.arange(ATT_TQ, dtype=jnp.int32)[None, :]
    period = 1 << (ATT_TK + ATT_TQ - 1).bit_length()
    j = jnp.arange(period, dtype=jnp.int32)
    t_minus_s = jnp.where(j < ATT_TQ, j, j - period)
    tiles = []
    for d in range(BIAS_TILES):
        by_offset = _bias_minus_far(rel_bias, -t_minus_s - d * ATT_TQ)
        bias = jnp.tile(by_offset, (1, ATT_TK))[:, :ATT_TK * (period - 1)]
        bias = bias.reshape(-1, ATT_TK, period - 1)[:, :, :ATT_TQ]
        allowed = (s // CHUNK - d * (ATT_TQ // CHUNK)) <= (t // CHUNK)
        tiles.append(jnp.where(allowed, bias, NEG_INF))
    tiles.append(jnp.full_like(tiles[0], NEG_INF))
    return jnp.stack(tiles, axis=1)


def _sample_bias_tile(rel_bias, t_new, n_near_past):
    q = jnp.arange(t_new, dtype=jnp.int32)[:, None]
    k = jnp.arange(n_near_past + t_new, dtype=jnp.int32)[None, :] - n_near_past
    return _bias_minus_far(rel_bias, k - q)


def _prepare_params(w_in, gla_w_alpha2, gla_b_alpha, norm_mix_g, norm_ffn_g, w_out, w_ffn_in, w_ffn_out,
                    final_norm_g):
    depth = w_in.shape[0]
    c_glr = _COLS["glr"][0]
    pad = jnp.zeros((depth, w_in.shape[1], GATE_PAD - GLA_GATE_RANK), w_in.dtype)
    w_in_p = jnp.concatenate([w_in[:, :, :c_glr + GLA_GATE_RANK], pad, w_in[:, :, c_glr + GLA_GATE_RANK:]], axis=2)
    wa = jnp.concatenate([gla_w_alpha2, jnp.zeros((depth, GATE_PAD - GLA_GATE_RANK, GLA_QK), gla_w_alpha2.dtype)],
                         axis=1)
    return dict(w_in=w_in_p.astype(BF16), wa=wa.astype(BF16), ba=gla_b_alpha.reshape(depth, 1, GLA_QK),
                norm_mix_g=norm_mix_g.reshape(depth, 1, D_MODEL), norm_ffn_g=norm_ffn_g.reshape(depth, 1, D_MODEL),
                w_out=w_out.astype(BF16), w_ffn_in=w_ffn_in.astype(BF16), w_ffn_out=w_ffn_out.astype(BF16),
                final_norm_g=final_norm_g.reshape(1, D_MODEL))


def _trunk(x, cache_k, cache_v, gla_state, params, *, prompt):
    batch, seq, _ = x.shape
    m = batch * seq
    x = x.reshape(m, D_MODEL)
    depth = params["w_in"].shape[0]
    tm_proj = min(512, m)
    tm_ffn = min(512, m)
    new_k, new_v, new_s = [], [], []
    if prompt:
        bias = _prompt_bias_tiles(params["rel_bias"])
    else:
        past = cache_k.shape[2]
        n_near_past = LANES
        bias = _sample_bias_tile(params["rel_bias"], seq, n_near_past)
        cache_k = cache_k.reshape(-1, DIFF_DV)
        cache_v = cache_v.reshape(-1, DIFF_DV)
    for l in range(depth):
        lam_init = 0.8 - 0.6 * math.exp(-0.3 * l)
        outs = _inproj(x, params["norm_mix_g"], params["w_in"], params["wa"], params["ba"],
                       layer=l, tm=tm_proj, with_attn_layouts=prompt)
        gq, gk, gv, gg, la, dq, dk, dv = outs[:8]
        s0 = jnp.zeros((batch, GLA_HEADS, GLA_DK, GLA_DV), F32) if gla_state is None else gla_state[l]
        o_gla, s_fin = _gla(gq, gk, gv, la, gg, s0, params["gla_norm_g"][l].reshape(1, GLA_DV),
                            batch=batch, seq=seq)
        dl = params["diff_lambda"][l]
        if prompt:
            dkb, dvt = outs[8:]
            o_diff = _attn_prompt(dq, dkb, dvt, bias, dl, params["diff_norm_g"][l].reshape(DIFF_DV, 1),
                                  lam_init=lam_init)
        else:
            o_diff = _attn_sample(dq, dk, dv, cache_k, cache_v, bias, dl,
                                  params["diff_norm_g"][l].reshape(1, DIFF_DV), layer=l, batch=batch, past=past,
                                  lam_init=lam_init, n_near_past=n_near_past)
        x = _mix_ffn(x, o_gla, o_diff, params["w_out"], params["norm_ffn_g"], params["w_ffn_in"],
                     params["w_ffn_out"], params["final_norm_g"], layer=l, final_norm=(l == depth - 1), tm=tm_ffn)
        new_k.append(dk.reshape(batch, seq, DIFF_HEADS, DIFF_DV))
        new_v.append(dv.reshape(batch, seq, DIFF_HEADS, DIFF_DV))
        new_s.append(s_fin)
    return x.reshape(batch, seq, D_MODEL), jnp.stack(new_k), jnp.stack(new_v), jnp.stack(new_s)


def kernel(x_prompt, x_sample, cache_diff_k, cache_diff_v, state_gla, w_in, gla_w_alpha2, gla_b_alpha, gla_norm_g,
           diff_lambda, diff_norm_g, w_out, norm_mix_g, norm_ffn_g, w_ffn_in, w_ffn_out, final_norm_g, rel_bias):
    params = _prepare_params(w_in, gla_w_alpha2, gla_b_alpha, norm_mix_g, norm_ffn_g, w_out, w_ffn_in, w_ffn_out,
                             final_norm_g)
    params.update(gla_norm_g=gla_norm_g, diff_lambda=diff_lambda, diff_norm_g=diff_norm_g, rel_bias=rel_bias)
    y_p, k_p, v_p, s_p = _trunk(x_prompt, None, None, None, params, prompt=True)
    y_s, k_s, v_s, s_s = _trunk(x_sample, cache_diff_k, cache_diff_v, state_gla, params, prompt=False)
    return (y_p, y_s, k_p, v_p, s_p, k_s, v_s, s_s)
```

```python
import functools
import math

import jax
import jax.numpy as jnp
from jax import lax
from jax.experimental import pallas as pl
from jax.experimental.pallas import tpu as pltpu

F32 = jnp.float32
BF16 = jnp.bfloat16

D_MODEL = 1024
CHUNK = 64
GLA_HEADS = 4
GLA_DK = 64
GLA_DV = 128
GLA_WIDTH = GLA_HEADS * GLA_DV
GLA_QK = GLA_HEADS * GLA_DK
GLA_GATE_RANK = 16
GLA_GATE_NORMALIZER = 16.0
DIFF_HEADS = 4
DIFF_DV = 128
DIFF_HEAD_DIM = 64
DIFF_WIDTH = DIFF_HEADS * DIFF_DV
T5_BUCKETS = 32
T5_MAX_DIST = 128
D_FF = 2816
NEG_INF = -1e30
RMS_EPS = 1e-6

LANES = 128
GATE_PAD = LANES
VMEM_LIMIT = 56 * 1024 * 1024

_COLS = {}
_off = 0
for _name, _w in (("gq", GLA_QK), ("gk", GLA_QK), ("gv", GLA_WIDTH), ("gg", GLA_WIDTH), ("glr", GATE_PAD),
                  ("dq", DIFF_WIDTH), ("dk", DIFF_WIDTH), ("dv", DIFF_WIDTH)):
    _COLS[_name] = (_off, _w)
    _off += _w
D_IN_PAD = _off

ATT_TQ = 256
ATT_TK = 512
ATT_STEPS_PER_QBLOCK = ATT_TK // ATT_TQ
ATT_CHUNK = 512
ATT_VT_ROWS = DIFF_DV + 16
ATT_HEADS_PER_STEP = 2
ATT_QBLOCKS_PER_STEP = 2
BIAS_TILES = -(-(ATT_TK + T5_MAX_DIST) // ATT_TQ) + 1
LOG2E = math.log2(math.e)
GLA_CHUNKS_PER_STEP = 8
GLA_CHUNKS_PER_TRIP = 2
FF_CHUNK = 512


def _nt_dot(a, b):
    return lax.dot_general(a, b, (((1,), (1,)), ((), ())), preferred_element_type=F32)


def _rms(x, g):
    return x * lax.rsqrt(jnp.mean(x * x, axis=-1, keepdims=True) + RMS_EPS) * g


def _inproj_kernel(x_ref, g_ref, w_ref, wa_ref, ba_ref,
                   gq_ref, gk_ref, gv_ref, gg_ref, la_ref, dq_ref, dk_ref, dv_ref, *attn_refs):
    xn = _rms(x_ref[...], g_ref[...]).astype(BF16)

    def proj(name):
        c0, w = _COLS[name]
        return jnp.dot(xn, w_ref[:, c0:c0 + w], preferred_element_type=F32)

    gq_ref[...] = proj("gq") * (GLA_DK ** -0.5)
    gk_ref[...] = proj("gk")
    gv_ref[...] = proj("gv")
    gg_ref[...] = proj("gg")
    z = jnp.dot(proj("glr").astype(BF16), wa_ref[...], preferred_element_type=F32) + ba_ref[...]
    la_ref[...] = (jnp.minimum(z, 0.0) - jnp.log1p(jnp.exp(-jnp.abs(z)))) / GLA_GATE_NORMALIZER
    dq_ref[...] = (proj("dq") * (DIFF_HEAD_DIM ** -0.5 * LOG2E)).astype(BF16)
    dk = proj("dk")
    dv = proj("dv")
    tm = dk.shape[0]
    for h in range(DIFF_HEADS):
        dk_ref[pl.ds(h, tm, stride=DIFF_HEADS), :] = dk[:, h * DIFF_DV:(h + 1) * DIFF_DV]
        dv_ref[pl.ds(h, tm, stride=DIFF_HEADS), :] = dv[:, h * DIFF_DV:(h + 1) * DIFF_DV]
    if attn_refs:
        dkb_ref, dvt_ref = attn_refs
        dkb_ref[...] = dk.astype(BF16)
        for r in range(tm // ATT_TK):
            vt = dv[r * ATT_TK:(r + 1) * ATT_TK, :].T
            for h in range(DIFF_HEADS):
                pad_row = lax.broadcasted_iota(jnp.int32, (ATT_VT_ROWS - DIFF_DV, ATT_TK), 0)
                ones_row = jnp.where(pad_row == 0, 1.0, 0.0)
                dvt_ref[h, r] = jnp.concatenate([vt[h * DIFF_DV:(h + 1) * DIFF_DV, :], ones_row],
                                                axis=0).astype(BF16)


def _inproj(x, g, w_in, wa, ba, *, layer, tm, with_attn_layouts):
    m = x.shape[0]
    row = lambda w: pl.BlockSpec((tm, w), lambda i: (i, 0))
    full = lambda a: _layer_block(a, layer)
    out_shape = [jax.ShapeDtypeStruct((m, GLA_QK), F32), jax.ShapeDtypeStruct((m, GLA_QK), F32),
                 jax.ShapeDtypeStruct((m, GLA_WIDTH), F32), jax.ShapeDtypeStruct((m, GLA_WIDTH), F32),
                 jax.ShapeDtypeStruct((m, GLA_QK), F32), jax.ShapeDtypeStruct((m, DIFF_WIDTH), BF16),
                 jax.ShapeDtypeStruct((m * DIFF_HEADS, DIFF_DV), F32),
                 jax.ShapeDtypeStruct((m * DIFF_HEADS, DIFF_DV), F32)]
    heads_rows = pl.BlockSpec((tm * DIFF_HEADS, DIFF_DV), lambda i: (i, 0))
    out_specs = [row(GLA_QK), row(GLA_QK), row(GLA_WIDTH), row(GLA_WIDTH), row(GLA_QK), row(DIFF_WIDTH),
                 heads_rows, heads_rows]
    if with_attn_layouts:
        out_shape += [jax.ShapeDtypeStruct((m, DIFF_WIDTH), BF16),
                      jax.ShapeDtypeStruct((DIFF_HEADS, m // ATT_TK, ATT_VT_ROWS, ATT_TK), BF16)]
        out_specs += [row(DIFF_WIDTH),
                      pl.BlockSpec((DIFF_HEADS, tm // ATT_TK, ATT_VT_ROWS, ATT_TK), lambda i: (0, i, 0, 0))]
    return pl.pallas_call(
        _inproj_kernel,
        grid=(m // tm,),
        in_specs=[row(D_MODEL), full(g), full(w_in), full(wa), full(ba)],
        out_specs=out_specs,
        out_shape=out_shape,
        compiler_params=pltpu.CompilerParams(dimension_semantics=("arbitrary",), vmem_limit_bytes=VMEM_LIMIT),
        name="inproj",
    )(x, g, w_in, wa, ba)


def _level_ref_rows(b, h):
    if h >= 8:
        pieces = []
        for blk in range(CHUNK // (2 * h)):
            ref = blk * 2 * h + h
            pieces.append(jnp.broadcast_to(b[ref:ref + 1, :], (2 * h, b.shape[1])))
        return jnp.concatenate(pieces, axis=0)
    b3 = b.reshape(CHUNK // 8, 8, b.shape[1])
    sub = lax.broadcasted_iota(jnp.int32, b3.shape, 1)
    pick = lambda s: jnp.broadcast_to(b3[:, s:s + 1, :], b3.shape)
    if h == 4:
        out = pick(4)
    elif h == 2:
        out = jnp.where(sub < 4, pick(2), pick(6))
    else:
        out = jnp.where(sub < 2, pick(1), jnp.where(sub < 4, pick(3), jnp.where(sub < 6, pick(5), pick(7))))
    return out.reshape(b.shape)


def _gla_chunk(q, k, v, la, gg, gn, s_prev):
    row = lax.broadcasted_iota(jnp.int32, (CHUNK, CHUNK), 0)
    col = lax.broadcasted_iota(jnp.int32, (CHUNK, CHUNK), 1)
    tril = (row >= col).astype(F32)
    b = jnp.dot(tril, la, preferred_element_type=F32, precision=lax.Precision.HIGHEST)
    b_last = b[CHUNK - 1:CHUNK, :]

    r256 = lax.broadcasted_iota(jnp.int32, (CHUNK, GLA_QK), 0)
    bd_r = lax.broadcasted_iota(jnp.int32, (GLA_QK, GLA_QK), 0) // GLA_DK
    bd_c = lax.broadcasted_iota(jnp.int32, (GLA_QK, GLA_QK), 1) // GLA_DK
    bd_mask = bd_r == bd_c
    t_idx = lax.broadcasted_iota(jnp.int32, (CHUNK, GLA_QK), 0)
    s_idx = lax.broadcasted_iota(jnp.int32, (CHUNK, GLA_QK), 1) % CHUNK

    def head_scores(a_bf, b_bf):
        bbd = jnp.where(bd_mask, jnp.concatenate([b_bf] * GLA_HEADS, axis=0), jnp.zeros((), BF16))
        return _nt_dot(a_bf, bbd)

    scores = jnp.where(t_idx == s_idx, head_scores(q.astype(BF16), k.astype(BF16)), 0.0)
    h = CHUNK // 2
    while h >= 1:
        upper = (r256 % (2 * h)) >= h
        bref = _level_ref_rows(b, h)
        e = jnp.exp(jnp.where(upper, b - bref, bref - b))
        a_l = jnp.where(upper, q * e, 0.0).astype(BF16)
        b_l = jnp.where(upper, 0.0, k * e).astype(BF16)
        same_block = (t_idx // (2 * h)) == (s_idx // (2 * h))
        scores = scores + jnp.where(same_block, head_scores(a_l, b_l), 0.0)
        h //= 2

    q_int = (q * jnp.exp(b)).astype(BF16)
    lhs = jnp.concatenate([scores.astype(BF16), q_int], axis=1)
    lane_head = (lax.broadcasted_iota(jnp.int32, lhs.shape, 1) % GLA_QK) // GLA_DK
    lhs_stack = jnp.concatenate(
        [jnp.where(lane_head == hd, lhs, jnp.zeros((), BF16)) for hd in range(GLA_HEADS)], axis=0)
    v_bf = v.astype(BF16)
    v_stack = jnp.concatenate([v_bf[:, hd * GLA_DV:(hd + 1) * GLA_DV] for hd in range(GLA_HEADS)], axis=0)
    rhs = jnp.concatenate([v_stack, s_prev.astype(BF16)], axis=0)
    o_stack = jnp.dot(lhs_stack, rhs, preferred_element_type=F32)

    k_dec = k * jnp.exp(b_last - b)
    kd_pad = jnp.concatenate([k_dec, jnp.broadcast_to(jnp.exp(b_last), (CHUNK, GLA_QK))], axis=0)
    kd_t = kd_pad.T
    dec_col = kd_t[:, CHUNK:CHUNK + 1]
    s_lane = lax.broadcasted_iota(jnp.int32, kd_t.shape, 1)
    kd_t_bf = jnp.where(s_lane < CHUNK, kd_t, 0.0).astype(BF16)
    zeros_v = jnp.zeros((LANES - CHUNK, GLA_DV), BF16)
    upd = []
    for hd in range(GLA_HEADS):
        v_h = jnp.concatenate([v_bf[:, hd * GLA_DV:(hd + 1) * GLA_DV], zeros_v], axis=0)
        upd.append(jnp.dot(kd_t_bf[hd * GLA_DK:(hd + 1) * GLA_DK, :], v_h, preferred_element_type=F32))
    s_new = dec_col * s_prev + jnp.concatenate(upd, axis=0)

    outs = []
    for hd in range(GLA_HEADS):
        o_h = o_stack[hd * CHUNK:(hd + 1) * CHUNK, :]
        outs.append(_rms(o_h, gn))
    o = jnp.concatenate(outs, axis=1)
    o = o * (gg * (1.0 / (1.0 + jnp.exp(-gg))))
    return o.astype(BF16), s_new


def _gla_kernel(q_ref, k_ref, v_ref, la_ref, gg_ref, s0_ref, gn_ref, o_ref, sfin_ref, s_scr, *, chunks):
    j = pl.program_id(1)

    @pl.when(j == 0)
    def _():
        for hd in range(GLA_HEADS):
            s_scr[hd * GLA_DK:(hd + 1) * GLA_DK, :] = s0_ref[0, hd]

    gn = gn_ref[...]

    unroll = GLA_CHUNKS_PER_TRIP if chunks % GLA_CHUNKS_PER_TRIP == 0 else 1

    def body(c, carry):
        s = s_scr[...]
        for u in range(unroll):
            r0 = pl.multiple_of((c * unroll + u) * CHUNK, CHUNK)
            rows = pl.ds(r0, CHUNK)
            o, s = _gla_chunk(q_ref[rows, :], k_ref[rows, :], v_ref[rows, :], la_ref[rows, :], gg_ref[rows, :],
                              gn, s)
            o_ref[rows, :] = o
        s_scr[...] = s
        return carry

    lax.fori_loop(0, chunks // unroll, body, 0)

    @pl.when(j == pl.num_programs(1) - 1)
    def _():
        for hd in range(GLA_HEADS):
            sfin_ref[0, hd] = s_scr[hd * GLA_DK:(hd + 1) * GLA_DK, :]


def _gla(gq, gk, gv, la, gg, s0, gn, *, batch, seq):
    n_chunks = seq // CHUNK
    chunks = min(GLA_CHUNKS_PER_STEP, n_chunks)
    steps = n_chunks // chunks
    tm = chunks * CHUNK
    row = lambda w: pl.BlockSpec((tm, w), lambda bi, j: (bi * steps + j, 0))
    st = pl.BlockSpec((1, GLA_HEADS, GLA_DK, GLA_DV), lambda bi, j: (bi, 0, 0, 0))
    return pl.pallas_call(
        functools.partial(_gla_kernel, chunks=chunks),
        grid=(batch, steps),
        in_specs=[row(GLA_QK), row(GLA_QK), row(GLA_WIDTH), row(GLA_QK), row(GLA_WIDTH), st,
                  pl.BlockSpec((1, GLA_DV), lambda bi, j: (0, 0))],
        out_specs=[row(GLA_WIDTH), st],
        out_shape=[jax.ShapeDtypeStruct((batch * seq, GLA_WIDTH), BF16),
                   jax.ShapeDtypeStruct((batch, GLA_HEADS, GLA_DK, GLA_DV), F32)],
        scratch_shapes=[pltpu.VMEM((GLA_QK, GLA_DV), F32)],
        compiler_params=pltpu.CompilerParams(dimension_semantics=("arbitrary", "arbitrary"),
                                             vmem_limit_bytes=VMEM_LIMIT),
        name="gla",
    )(gq, gk, gv, la, gg, s0, gn)


def _lambda(dl, lam_init):
    s01 = jnp.sum(dl[0:1, :] * dl[1:2, :], axis=1, keepdims=True)
    s23 = jnp.sum(dl[2:3, :] * dl[3:4, :], axis=1, keepdims=True)
    return jnp.exp(s01) - jnp.exp(s23) + lam_init


def _split_maps(q):
    lane = lax.broadcasted_iota(jnp.int32, q.shape, 1)
    zero = jnp.zeros((), q.dtype)
    return jnp.concatenate([jnp.where(lane < DIFF_HEAD_DIM, q, zero), jnp.where(lane >= DIFF_HEAD_DIM, q, zero)],
                           axis=0)


def _attn_prompt_kernel(q_ref, k_ref, vt_ref, bias_ref, dl_ref, gn_ref, o_ref, wq_scr, s_a, s_b, p_a, p_b,
                        acc_scr, *, lam_init):
    i = pl.program_id(1)
    chains = [(hd, qb) for hd in range(acc_scr.shape[0] // ATT_QBLOCKS_PER_STEP)
              for qb in range(ATT_QBLOCKS_PER_STEP)]
    n_steps = (i * ATT_QBLOCKS_PER_STEP) // ATT_STEPS_PER_QBLOCK + 1
    n_pairs = (n_steps + 1) // 2
    q_rows = lambda qb: slice(qb * ATT_TQ, (qb + 1) * ATT_TQ)
    h_cols = lambda hd: slice(hd * DIFF_DV, (hd + 1) * DIFF_DV)
    for g, (hd, qb) in enumerate(chains):
        qt = q_ref[q_rows(qb), h_cols(hd)].astype(F32).T
        row = lax.broadcasted_iota(jnp.int32, qt.shape, 0)
        wq_scr[g] = jnp.concatenate([jnp.where(row < DIFF_HEAD_DIM, qt, 0.0),
                                     jnp.where(row >= DIFF_HEAD_DIM, qt, 0.0)], axis=1).astype(BF16)
    acc_scr[...] = jnp.zeros_like(acc_scr)
    p_b[...] = jnp.zeros_like(p_b)

    def qk_chunk(g, t, c, s_scr):
        hd, qb = chains[g]
        tc = jnp.minimum(t, n_steps - 1)
        k0 = pl.multiple_of(tc * ATT_TK + c * ATT_CHUNK, ATT_CHUNK)
        behind = i * ATT_QBLOCKS_PER_STEP + qb - ATT_STEPS_PER_QBLOCK * tc
        tile = jnp.where(t < n_steps, jnp.minimum(behind, BIAS_TILES - 1), BIAS_TILES)
        rows = slice(c * ATT_CHUNK, (c + 1) * ATT_CHUNK)
        bias = bias_ref[hd, tile, rows, :]
        s = (jnp.dot(k_ref[pl.ds(k0, ATT_CHUNK), h_cols(hd)], wq_scr[g],
                     preferred_element_type=F32) + jnp.concatenate([bias, bias], axis=1))
        s_scr[g, rows, :2 * ATT_TQ] = s
        return jnp.max(s, axis=0, keepdims=True)

    def softmax_chunk(g, c, s_scr, p_scr, m_new):
        rows = slice(c * ATT_CHUNK, (c + 1) * ATT_CHUNK)
        p_scr[g, rows, :2 * ATT_TQ] = jnp.exp2(s_scr[g, rows, :2 * ATT_TQ] - m_new).astype(BF16)

    def pv_chunk(g, t, c, p_scr, alpha):
        tc = jnp.clip(t, 0, n_steps - 1)
        cols = slice(c * ATT_CHUNK, (c + 1) * ATT_CHUNK)
        d = jnp.dot(vt_ref[chains[g][0], tc, :, cols], p_scr[g, cols, :2 * ATT_TQ], preferred_element_type=F32)
        if c == 0:
            acc_scr[g] = acc_scr[g] * alpha + d
        else:
            acc_scr[g] += d

    def half_step(g, t, s_cur, p_cur, s_nxt, p_prev, m, mb_cur, alpha_prev):
        m_new = jnp.maximum(m, mb_cur)
        alpha = jnp.exp2(m - m_new)
        mb_nxt = None
        for c in range(ATT_TK // ATT_CHUNK):
            softmax_chunk(g, c, s_cur, p_cur, m_new)
            pv_chunk(g, t - 1, c, p_prev, alpha_prev)
            mbc = qk_chunk(g, t + 1, c, s_nxt)
            mb_nxt = mbc if mb_nxt is None else jnp.maximum(mb_nxt, mbc)
        return m_new, mb_nxt, alpha

    def body(jj, carry):
        t = 2 * jj
        out = []
        for g in range(len(chains)):
            m, mb_a, alpha_b = carry[g]
            m, mb_b, alpha_a = half_step(g, t, s_a, p_a, s_b, p_b, m, mb_a, alpha_b)
            m, mb_a, alpha_b = half_step(g, t + 1, s_b, p_b, s_a, p_a, m, mb_b, alpha_a)
            out.append((m, mb_a, alpha_b))
        return tuple(out)

    init = []
    for g in range(len(chains)):
        mb0 = None
        for c in range(ATT_TK // ATT_CHUNK):
            mbc = qk_chunk(g, 0, c, s_a)
            mb0 = mbc if mb0 is None else jnp.maximum(mb0, mbc)
        init.append((jnp.full((1, 2 * ATT_TQ), NEG_INF, F32), mb0, jnp.ones((1, 2 * ATT_TQ), F32)))
    final = lax.fori_loop(0, n_pairs, body, tuple(init))

    lam = _lambda(dl_ref[...], lam_init)
    for g, (hd, qb) in enumerate(chains):
        alpha_b = final[g][2]
        for c in range(ATT_TK // ATT_CHUNK):
            pv_chunk(g, 2 * n_pairs - 1, c, p_b, alpha_b)
        acc = acc_scr[g]
        o = acc[:DIFF_DV, :] / acc[DIFF_DV:DIFF_DV + 1, :]
        od = o[:, :ATT_TQ] - lam * o[:, ATT_TQ:]
        y = od * lax.rsqrt(jnp.mean(od * od, axis=0, keepdims=True) + RMS_EPS) * gn_ref[...] * (1.0 - lam_init)
        o_ref[q_rows(qb), h_cols(hd)] = y.T.astype(BF16)


def _attn_prompt(dq, dkb, dvt, bias, dl, gn_col, *, lam_init):
    t = dq.shape[0]
    g = ATT_HEADS_PER_STEP
    qblocks = ATT_QBLOCKS_PER_STEP
    tq = qblocks * ATT_TQ
    assert t % ATT_TK == 0 and ATT_TK % tq == 0 and t % tq == 0 and DIFF_HEADS % g == 0
    n_chains = g * qblocks
    tile = lambda dtype: pltpu.VMEM((n_chains, ATT_TK, 2 * ATT_TQ + LANES), dtype)
    once = dict(pipeline_mode=pl.Buffered(1))
    return pl.pallas_call(
        functools.partial(_attn_prompt_kernel, lam_init=lam_init),
        grid=(DIFF_HEADS // g, t // tq),
        in_specs=[pl.BlockSpec((tq, g * DIFF_DV), lambda h, i: (i, h)),
                  pl.BlockSpec((t, g * DIFF_DV), lambda h, i: (0, h), **once),
                  pl.BlockSpec((g, t // ATT_TK, ATT_VT_ROWS, ATT_TK), lambda h, i: (h, 0, 0, 0), **once),
                  pl.BlockSpec((g,) + bias.shape[1:], lambda h, i: (h, 0, 0, 0), **once),
                  pl.BlockSpec(dl.shape, lambda h, i: (0, 0)),
                  pl.BlockSpec(gn_col.shape, lambda h, i: (0, 0))],
        out_specs=pl.BlockSpec((tq, g * DIFF_DV), lambda h, i: (i, h)),
        out_shape=jax.ShapeDtypeStruct((t, DIFF_WIDTH), BF16),
        scratch_shapes=[pltpu.VMEM((n_chains, DIFF_DV, 2 * ATT_TQ), BF16), tile(F32), tile(F32), tile(BF16),
                        tile(BF16), pltpu.VMEM((n_chains, ATT_VT_ROWS, 2 * ATT_TQ), F32)],
        compiler_params=pltpu.CompilerParams(dimension_semantics=("arbitrary", "arbitrary"),
                                             vmem_limit_bytes=VMEM_LIMIT),
        name="attn_prompt",
    )(dq, dkb, dvt, bias, dl, gn_col)


def _attn_sample_kernel(q_ref, kn_ref, vn_ref, kc_ref, vc_ref, bias_ref, dl_ref, gn_ref, o_ref, acc_scr, *,
                        lam_init, n_near_past):
    t = q_ref.shape[0]
    past = kc_ref.shape[0] // DIFF_HEADS
    n_far = past - n_near_past
    lam = _lambda(dl_ref[...], lam_init)
    for h in range(DIFF_HEADS):
        head_rows = lambda ref, first, n: ref[pl.ds(first * DIFF_HEADS + h, n, stride=DIFF_HEADS), :]
        wq = _split_maps(q_ref[:, h * DIFF_DV:(h + 1) * DIFF_DV])
        k_far = head_rows(kc_ref, 0, n_far).astype(BF16)
        v_far = head_rows(vc_ref, 0, n_far).astype(BF16)
        k_near = jnp.concatenate([head_rows(kc_ref, n_far, n_near_past), head_rows(kn_ref, 0, t)],
                                 axis=0).astype(BF16)
        v_near = jnp.concatenate([head_rows(vc_ref, n_far, n_near_past), head_rows(vn_ref, 0, t)],
                                 axis=0).astype(BF16)
        bias = bias_ref[h]
        s_far = _nt_dot(wq, k_far)
        s_near = _nt_dot(wq, k_near) + jnp.concatenate([bias, bias], axis=0)
        m = jnp.maximum(jnp.max(s_far, axis=1, keepdims=True), jnp.max(s_near, axis=1, keepdims=True))
        p_far = jnp.exp2(s_far - m)
        p_near = jnp.exp2(s_near - m)
        l = jnp.sum(p_far, axis=1, keepdims=True) + jnp.sum(p_near, axis=1, keepdims=True)
        acc_scr[...] = jnp.dot(p_far.astype(BF16), v_far, preferred_element_type=F32)
        acc_scr[...] += jnp.dot(p_near.astype(BF16), v_near, preferred_element_type=F32)
        o = acc_scr[...] / l
        od = o[:t, :] - lam * o[t:, :]
        o_ref[:, h * DIFF_DV:(h + 1) * DIFF_DV] = (_rms(od, gn_ref[...]) * (1.0 - lam_init)).astype(BF16)


def _attn_sample(dq, dk, dv, cache_k, cache_v, bias, dl, gn_row, *, layer, batch, past, lam_init, n_near_past):
    t = dq.shape[0] // batch
    new = lambda: pl.BlockSpec((t * DIFF_HEADS, DIFF_DV), lambda b: (b, 0))
    cache = lambda: pl.BlockSpec((past * DIFF_HEADS, DIFF_DV), lambda b: (layer * batch + b, 0))
    wide = lambda: pl.BlockSpec((t, DIFF_WIDTH), lambda b: (b, 0))
    const = lambda a: pl.BlockSpec(a.shape, lambda b: (0,) * a.ndim)
    return pl.pallas_call(
        functools.partial(_attn_sample_kernel, lam_init=lam_init, n_near_past=n_near_past),
        grid=(batch,),
        in_specs=[wide(), new(), new(), cache(), cache(), const(bias), const(dl), const(gn_row)],
        out_specs=wide(),
        out_shape=jax.ShapeDtypeStruct((batch * t, DIFF_WIDTH), BF16),
        scratch_shapes=[pltpu.VMEM((2 * t, DIFF_DV), F32)],
        compiler_params=pltpu.CompilerParams(dimension_semantics=("arbitrary",), vmem_limit_bytes=VMEM_LIMIT),
        name="attn_sample",
    )(dq, dk, dv, cache_k, cache_v, bias, dl, gn_row)


def _mix_ffn_kernel(x_ref, og_ref, od_ref, wo_ref, g_ref, wi_ref, wf_ref, gf_ref, y_ref, *, final_norm):
    x = (x_ref[...]
         + jnp.dot(og_ref[...], wo_ref[:GLA_WIDTH, :], preferred_element_type=F32)
         + jnp.dot(od_ref[...], wo_ref[GLA_WIDTH:, :], preferred_element_type=F32))
    hn = _rms(x, g_ref[...]).astype(BF16)
    y_ref[...] = x
    for c0 in range(0, D_FF, FF_CHUNK):
        c1 = min(c0 + FF_CHUNK, D_FF)
        gate = jnp.dot(hn, wi_ref[:, c0:c1], preferred_element_type=F32)
        up = jnp.dot(hn, wi_ref[:, D_FF + c0:D_FF + c1], preferred_element_type=F32)
        hmid = (gate * (1.0 / (1.0 + jnp.exp(-gate))) * up).astype(BF16)
        y_ref[...] += jnp.dot(hmid, wf_ref[c0:c1, :], preferred_element_type=F32)
    if final_norm:
        y_ref[...] = _rms(y_ref[...], gf_ref[...])


def _layer_block(a, layer):
    return pl.BlockSpec((None,) + a.shape[1:], lambda i: (layer,) + (0,) * (a.ndim - 1),
                        pipeline_mode=pl.Buffered(1))


def _mix_ffn(x, og, od, w_out, g, w_ffn_in, w_ffn_out, g_final, *, layer, final_norm, tm):
    m = x.shape[0]
    row = lambda w: pl.BlockSpec((tm, w), lambda i: (i, 0))
    return pl.pallas_call(
        functools.partial(_mix_ffn_kernel, final_norm=final_norm),
        grid=(m // tm,),
        in_specs=[row(D_MODEL), row(GLA_WIDTH), row(DIFF_WIDTH), _layer_block(w_out, layer), _layer_block(g, layer),
                  _layer_block(w_ffn_in, layer), _layer_block(w_ffn_out, layer),
                  pl.BlockSpec(g_final.shape, lambda i: (0, 0))],
        out_specs=row(D_MODEL),
        out_shape=jax.ShapeDtypeStruct((m, D_MODEL), F32),
        compiler_params=pltpu.CompilerParams(dimension_semantics=("arbitrary",), vmem_limit_bytes=VMEM_LIMIT),
        name="mix_ffn",
    )(x, og, od, w_out, g, w_ffn_in, w_ffn_out, g_final)


def _t5_bucket(rel):
    nb = T5_BUCKETS // 2
    max_exact = nb // 2
    ret = jnp.where(rel > 0, nb, 0)
    n = jnp.abs(rel)
    nf = jnp.maximum(n, 1).astype(F32)
    large = max_exact + (jnp.log(nf / max_exact) / math.log(T5_MAX_DIST / max_exact)
                         * (nb - max_exact)).astype(jnp.int32)
    large = jnp.minimum(large, nb - 1)
    return ret + jnp.where(n < max_exact, n, large)


def _bias_minus_far(rel_bias, rel):
    table = rel_bias.astype(F32)
    table = (table - table[_t5_bucket(jnp.full((), -T5_MAX_DIST, jnp.int32))]) * LOG2E
    hit = _t5_bucket(rel)[..., None, None] == jnp.arange(T5_BUCKETS, dtype=jnp.int32)[:, None]
    return jnp.moveaxis(jnp.sum(jnp.where(hit, table, 0.0), axis=-2), -1, 0)


def _prompt_bias_tiles(rel_bias):
    s = jnp.arange(ATT_TK, dtype=jnp.int32)[:, None]
    t = jnp.arange(ATT_TQ, dtype=jnp.int32)[None, :]
    period = 1 << (ATT_TK + ATT_TQ - 1).bit_length()
    j = jnp.arange(period, dtype=jnp.int32)
    t_minus_s = jnp.where(j < ATT_TQ, j, j - period)
    tiles = []
    for d in range(BIAS_TILES):
        by_offset = _bias_minus_far(rel_bias, -t_minus_s - d * ATT_TQ)
        bias = jnp.tile(by_offset, (1, ATT_TK))[:, :ATT_TK * (period - 1)]
        bias = bias.reshape(-1, ATT_TK, period - 1)[:, :, :ATT_TQ]
        allowed = (s // CHUNK - d * (ATT_TQ // CHUNK)) <= (t // CHUNK)
        tiles.append(jnp.where(allowed, bias, NEG_INF))
    tiles.append(jnp.full_like(tiles[0], NEG_INF))
    return jnp.stack(tiles, axis=1)


def _sample_bias_tile(rel_bias, t_new, n_near_past):
    q = jnp.arange(t_new, dtype=jnp.int32)[:, None]
    k = jnp.arange(n_near_past + t_new, dtype=jnp.int32)[None, :] - n_near_past
    return _bias_minus_far(rel_bias, k - q)


def _prepare_params(w_in, gla_w_alpha2, gla_b_alpha, norm_mix_g, norm_ffn_g, w_out, w_ffn_in, w_ffn_out,
                    final_norm_g):
    depth = w_in.shape[0]
    c_glr = _COLS["glr"][0]
    pad = jnp.zeros((depth, w_in.shape[1], GATE_PAD - GLA_GATE_RANK), w_in.dtype)
    w_in_p = jnp.concatenate([w_in[:, :, :c_glr + GLA_GATE_RANK], pad, w_in[:, :, c_glr + GLA_GATE_RANK:]], axis=2)
    wa = jnp.concatenate([gla_w_alpha2, jnp.zeros((depth, GATE_PAD - GLA_GATE_RANK, GLA_QK), gla_w_alpha2.dtype)],
                         axis=1)
    return dict(w_in=w_in_p.astype(BF16), wa=wa.astype(BF16), ba=gla_b_alpha.reshape(depth, 1, GLA_QK),
                norm_mix_g=norm_mix_g.reshape(depth, 1, D_MODEL), norm_ffn_g=norm_ffn_g.reshape(depth, 1, D_MODEL),
                w_out=w_out.astype(BF16), w_ffn_in=w_ffn_in.astype(BF16), w_ffn_out=w_ffn_out.astype(BF16),
                final_norm_g=final_norm_g.reshape(1, D_MODEL))


def _trunk(x, cache_k, cache_v, gla_state, params, *, prompt):
    batch, seq, _ = x.shape
    m = batch * seq
    x = x.reshape(m, D_MODEL)
    depth = params["w_in"].shape[0]
    tm_proj = min(512, m)
    tm_ffn = min(512, m)
    new_k, new_v, new_s = [], [], []
    if prompt:
        bias = _prompt_bias_tiles(params["rel_bias"])
    else:
        past = cache_k.shape[2]
        n_near_past = LANES
        bias = _sample_bias_tile(params["rel_bias"], seq, n_near_past)
        cache_k = cache_k.reshape(-1, DIFF_DV)
        cache_v = cache_v.reshape(-1, DIFF_DV)
    for l in range(depth):
        lam_init = 0.8 - 0.6 * math.exp(-0.3 * l)
        outs = _inproj(x, params["norm_mix_g"], params["w_in"], params["wa"], params["ba"],
                       layer=l, tm=tm_proj, with_attn_layouts=prompt)
        gq, gk, gv, gg, la, dq, dk, dv = outs[:8]
        s0 = jnp.zeros((batch, GLA_HEADS, GLA_DK, GLA_DV), F32) if gla_state is None else gla_state[l]
        o_gla, s_fin = _gla(gq, gk, gv, la, gg, s0, params["gla_norm_g"][l].reshape(1, GLA_DV),
                            batch=batch, seq=seq)
        dl = params["diff_lambda"][l]
        if prompt:
            dkb, dvt = outs[8:]
            o_diff = _attn_prompt(dq, dkb, dvt, bias, dl, params["diff_norm_g"][l].reshape(DIFF_DV, 1),
                                  lam_init=lam_init)
        else:
            o_diff = _attn_sample(dq, dk, dv, cache_k, cache_v, bias, dl,
                                  params["diff_norm_g"][l].reshape(1, DIFF_DV), layer=l, batch=batch, past=past,
                                  lam_init=lam_init, n_near_past=n_near_past)
        x = _mix_ffn(x, o_gla, o_diff, params["w_out"], params["norm_ffn_g"], params["w_ffn_in"],
                     params["w_ffn_out"], params["final_norm_g"], layer=l, final_norm=(l == depth - 1), tm=tm_ffn)
        new_k.append(dk.reshape(batch, seq, DIFF_HEADS, DIFF_DV))
        new_v.append(dv.reshape(batch, seq, DIFF_HEADS, DIFF_DV))
        new_s.append(s_fin)
    return x.reshape(batch, seq, D_MODEL), jnp.stack(new_k), jnp.stack(new_v), jnp.stack(new_s)


def kernel(x_prompt, x_sample, cache_diff_k, cache_diff_v, state_gla, w_in, gla_w_alpha2, gla_b_alpha, gla_norm_g,
           diff_lambda, diff_norm_g, w_out, norm_mix_g, norm_ffn_g, w_ffn_in, w_ffn_out, final_norm_g, rel_bias):
    params = _prepare_params(w_in, gla_w_alpha2, gla_b_alpha, norm_mix_g, norm_ffn_g, w_out, w_ffn_in, w_ffn_out,
                             final_norm_g)
    params.update(gla_norm_g=gla_norm_g, diff_lambda=diff_lambda, diff_norm_g=diff_norm_g, rel_bias=rel_bias)
    y_p, k_p, v_p, s_p = _trunk(x_prompt, None, None, None, params, prompt=True)
    y_s, k_s, v_s, s_s = _trunk(x_sample, cache_diff_k, cache_diff_v, state_gla, params, prompt=False)
    return (y_p, y_s, k_p, v_p, s_p, k_s, v_s, s_s)
```
